```python
import functools
import jax, jax.numpy as jnp
from jax import lax
import numpy as np

D_MODEL = 1024
BATCH = 16
SEQ = 2048
DEPTH = 1
DEC_BATCH = 128
DEC_SEQ = 1
PAST_LEN = 16384
PAGE_SIZE = 128

MLA_HEADS = 8
MLA_NOPE_DIM = 64
MLA_ROPE_DIM = 32
MLA_V_DIM = 64
MLA_Q_LORA = 256
MLA_KV_LORA = 128
ROPE_BASE = 10000.0
MLA_SCALE = (MLA_NOPE_DIM + MLA_ROPE_DIM) ** -0.5
MLA_OUT_WIDTH = MLA_HEADS * MLA_V_DIM
SB_HEADS = 8
SB_HEAD_DIM = 64
SB_WIDTH = SB_HEADS * SB_HEAD_DIM
SB_SCALE = SB_HEAD_DIM ** -0.5
IN_SIZES = (MLA_Q_LORA, MLA_KV_LORA, MLA_ROPE_DIM, SB_WIDTH, SB_WIDTH, SB_WIDTH, D_MODEL, D_MODEL)
IN_WIDTH = MLA_Q_LORA + MLA_KV_LORA + MLA_ROPE_DIM + 3 * SB_WIDTH + 2 * D_MODEL
D_FF = 2816
CONV_W = 3
Q_BLOCK = 128
NORM_EPS = 1e-6

kernel_name = 'hybrid_mla_stickbreaking_convffn_step'


def _rmsnorm(x, g):
    xf = x.astype(jnp.float32)
    y = xf * lax.rsqrt(jnp.mean(xf * xf, axis=-1, keepdims=True) + NORM_EPS)
    return (y * g.astype(jnp.float32)).astype(x.dtype)


def _rope(x, pos):
    half = MLA_ROPE_DIM // 2
    inv = ROPE_BASE ** (-jnp.arange(half, dtype=jnp.float32) / half)
    ang = pos.astype(jnp.float32)[:, None] * inv[None, :]
    shape = (ang.shape[0],) + (1,) * (x.ndim - 3) + (half,)
    cos = jnp.cos(ang).reshape(shape)
    sin = jnp.sin(ang).reshape(shape)
    xf = x.astype(jnp.float32)
    x1, x2 = xf[..., :half], xf[..., half:]
    return jnp.concatenate([x1 * cos - x2 * sin, x1 * sin + x2 * cos], axis=-1).astype(x.dtype)


def _project(xn, pos, w_in, q_norm, kv_norm, w_uq, w_ukv):
    B, T, _ = xn.shape
    offsets = []
    acc = 0
    for s in IN_SIZES[:-1]:
        acc += s
        offsets.append(acc)
    cq, ckv, kr, sq, sk, sv, ga, gb = jnp.split(xn @ w_in, offsets, axis=-1)
    q = (_rmsnorm(cq, q_norm) @ w_uq).reshape(B, T, MLA_HEADS, MLA_NOPE_DIM + MLA_ROPE_DIM)
    q_nope, q_rope = q[..., :MLA_NOPE_DIM], q[..., MLA_NOPE_DIM:]
    w_ukv3 = w_ukv.reshape(MLA_KV_LORA, MLA_HEADS, MLA_NOPE_DIM + MLA_V_DIM)
    w_uk, w_uv = w_ukv3[..., :MLA_NOPE_DIM], w_ukv3[..., MLA_NOPE_DIM:]
    q_lat = jnp.einsum('bthn,rhn->bthr', q_nope, w_uk) * MLA_SCALE
    q_rope = _rope(q_rope, pos) * MLA_SCALE
    c_kv = _rmsnorm(ckv, kv_norm)
    k_rope = _rope(kr, pos)
    sq = sq.reshape(B, T, SB_HEADS, SB_HEAD_DIM) * SB_SCALE
    sk = sk.reshape(B, T, SB_HEADS, SB_HEAD_DIM)
    sv = sv.reshape(B, T, SB_HEADS, SB_HEAD_DIM)
    return q_lat, q_rope, c_kv, k_rope, sq, sk, sv, ga, gb, w_uv


def _mla_attend(q_lat, q_rope, c_kv, k_rope, mask):
    s = (jnp.einsum('bqhr,bkr->bhqk', q_lat, c_kv)
         + jnp.einsum('bqhd,bkd->bhqk', q_rope, k_rope)).astype(jnp.float32)
    p = jax.nn.softmax(jnp.where(mask, s, -jnp.inf), axis=-1)
    return jnp.einsum('bhqk,bkr->bqhr', p.astype(c_kv.dtype), c_kv)


def _stick_breaking(q, k, v, mask, log_keep_after):
    z = jnp.einsum('bqhd,bkhd->bhqk', q, k).astype(jnp.float32)
    log_keep = jnp.where(mask, jax.nn.log_sigmoid(-z), 0.0)
    after = lax.cumsum(log_keep, axis=3, reverse=True) - log_keep + log_keep_after[..., None]
    log_a = jnp.where(mask, jax.nn.log_sigmoid(z) + after, -jnp.inf)
    out = jnp.einsum('bhqk,bkhd->bqhd', jnp.exp(log_a), v.astype(jnp.float32))
    return out, log_keep_after + jnp.sum(log_keep, axis=-1)


def _to_blocks(x):
    B, S = x.shape[:2]
    return x.reshape((B, S // Q_BLOCK, Q_BLOCK) + x.shape[2:]).swapaxes(0, 1)


def _from_blocks(x):
    NB, B, QB = x.shape[:3]
    return x.swapaxes(0, 1).reshape((B, NB * QB) + x.shape[3:])


def _prompt_attention(q_lat, q_rope, c_kv, k_rope, sq, sk, sv):
    B, S = c_kv.shape[:2]
    pos = jnp.arange(S)

    def one_block(args):
        ql, qr, qs, qp = args
        o_lat = _mla_attend(ql, qr, c_kv, k_rope, pos[None, :] <= qp[:, None])
        zero = jnp.zeros((B, SB_HEADS, qp.shape[0]), jnp.float32)
        o_sb, _ = _stick_breaking(qs, sk, sv, pos[None, :] < qp[:, None], zero)
        return o_lat, o_sb

    o_lat, o_sb = lax.map(one_block, (_to_blocks(q_lat), _to_blocks(q_rope), _to_blocks(sq),
                                      pos.reshape(-1, Q_BLOCK)))
    return _from_blocks(o_lat), _from_blocks(o_sb)


def _sample_attention(q_lat, q_rope, c_kv, k_rope, sq, sk, sv, lat_pool, kr_pool, sbk_pool, sbv_pool, page_table):
    DB, T = c_kv.shape[:2]
    n_pages = page_table.shape[1]
    past = n_pages * PAGE_SIZE
    q_pos = past + jnp.arange(T)
    k_pos = jnp.arange(past + T)
    c_all = jnp.concatenate([lat_pool[page_table].reshape(DB, past, MLA_KV_LORA), c_kv], axis=1)
    kr_all = jnp.concatenate([kr_pool[page_table].reshape(DB, past, MLA_ROPE_DIM), k_rope], axis=1)
    o_lat = _mla_attend(q_lat, q_rope, c_all, kr_all, k_pos[None, :] <= q_pos[:, None])
    t_idx = jnp.arange(T)
    o_new, keep = _stick_breaking(sq, sk, sv, t_idx[None, :] < t_idx[:, None],
                                  jnp.zeros((DB, SB_HEADS, T), jnp.float32))
    page_mask = jnp.ones((T, PAGE_SIZE), dtype=bool)

    def body(carry, phys):
        acc, keep_after = carry
        o, keep_after = _stick_breaking(sq, sbk_pool[phys], sbv_pool[phys], page_mask, keep_after)
        return (acc + o, keep_after), None

    (o_sb, _), _ = lax.scan(body, (o_new, keep), page_table.T[::-1])
    return o_lat, o_sb


def _attention_branches(xn, pos, w_in, q_norm, kv_norm, w_uq, w_ukv, w_mla_o, w_sb_o, w_out, attend):
    B, T, _ = xn.shape
    q_lat, q_rope, c_kv, k_rope, sq, sk, sv, ga, gb, w_uv = _project(xn, pos, w_in, q_norm, kv_norm, w_uq, w_ukv)
    o_lat, o_sb = attend(q_lat, q_rope, c_kv, k_rope, sq, sk, sv)
    o_mla = jnp.einsum('bthr,rhv->bthv', o_lat.astype(xn.dtype), w_uv).reshape(B, T, MLA_OUT_WIDTH) @ w_mla_o
    o_sbp = o_sb.astype(xn.dtype).reshape(B, T, SB_WIDTH) @ w_sb_o
    y = (jax.nn.sigmoid(ga) * o_mla + jax.nn.sigmoid(gb) * o_sbp) @ w_out
    return y, c_kv, k_rope, sk, sv


def _conv_ffn(xn, prev, w_up, conv_w, conv_b, w_down):
    T = xn.shape[1]
    u = xn @ w_up
    u_pad = jnp.concatenate([prev.astype(u.dtype), u], axis=1)
    c = conv_b
    for k in range(CONV_W):
        c = c + conv_w[k] * u_pad[:, k:k + T]
    a, b = jnp.split(c, 2, axis=-1)
    return (jax.nn.silu(a) * b) @ w_down, u_pad[:, T:]


def setup_inputs(seed: int = 0) -> dict:
    key = jax.random.key(seed)
    ks = jax.random.split(key, 24)
    n_pages = PAST_LEN // PAGE_SIZE
    n_used = DEC_BATCH * n_pages
    n_pool = n_used + n_used // 4

    def nrm(k, shape, scale):
        return scale * jax.random.normal(k, shape, jnp.float32)

    page_table = jax.random.permutation(ks[0], n_pool)[:n_used].reshape(DEC_BATCH, n_pages).astype(jnp.int32)
    return {
        'x_prompt': nrm(ks[1], (BATCH, SEQ, D_MODEL), 1.0),
        'x_sample': nrm(ks[2], (DEC_BATCH, DEC_SEQ, D_MODEL), 1.0),
        'cache_latent': nrm(ks[3], (DEPTH, n_pool, PAGE_SIZE, MLA_KV_LORA), 1.0),
        'cache_krope': nrm(ks[4], (DEPTH, n_pool, PAGE_SIZE, MLA_ROPE_DIM), 1.0),
        'cache_sb_k': nrm(ks[5], (DEPTH, n_pool, PAGE_SIZE, SB_HEADS, SB_HEAD_DIM), 1.0),
        'cache_sb_v': nrm(ks[6], (DEPTH, n_pool, PAGE_SIZE, SB_HEADS, SB_HEAD_DIM), 1.0),
        'state_conv': nrm(ks[7], (DEPTH, DEC_BATCH, CONV_W - 1, 2 * D_FF), 1.0),
        'page_table': page_table,
        'attn_norm': 1.0 + nrm(ks[8], (DEPTH, D_MODEL), 0.02),
        'w_in': nrm(ks[9], (DEPTH, D_MODEL, IN_WIDTH), D_MODEL ** -0.5),
        'q_norm': 1.0 + nrm(ks[10], (DEPTH, MLA_Q_LORA), 0.02),
        'kv_norm': 1.0 + nrm(ks[11], (DEPTH, MLA_KV_LORA), 0.02),
        'w_uq': nrm(ks[12], (DEPTH, MLA_Q_LORA, MLA_HEADS * (MLA_NOPE_DIM + MLA_ROPE_DIM)), MLA_Q_LORA ** -0.5),
        'w_ukv': nrm(ks[13], (DEPTH, MLA_KV_LORA, MLA_HEADS * (MLA_NOPE_DIM + MLA_V_DIM)), MLA_KV_LORA ** -0.5),
        'w_mla_o': nrm(ks[14], (DEPTH, MLA_OUT_WIDTH, D_MODEL), MLA_OUT_WIDTH ** -0.5),
        'w_sb_o': nrm(ks[15], (DEPTH, SB_WIDTH, D_MODEL), SB_WIDTH ** -0.5),
        'w_out': nrm(ks[16], (DEPTH, D_MODEL, D_MODEL), D_MODEL ** -0.5),
        'ffn_norm': 1.0 + nrm(ks[17], (DEPTH, D_MODEL), 0.02),
        'w_up': nrm(ks[18], (DEPTH, D_MODEL, 2 * D_FF), D_MODEL ** -0.5),
        'conv_w': nrm(ks[19], (DEPTH, CONV_W, 2 * D_FF), CONV_W ** -0.5),
        'conv_b': nrm(ks[20], (DEPTH, 2 * D_FF), 0.01),
        'w_down': nrm(ks[21], (DEPTH, D_FF, D_MODEL), D_FF ** -0.5),
        'final_norm': 1.0 + nrm(ks[22], (D_MODEL,), 0.02),
    }


def reference(x_prompt, x_sample, cache_latent, cache_krope, cache_sb_k, cache_sb_v, state_conv, page_table,
              attn_norm, w_in, q_norm, kv_norm, w_uq, w_ukv, w_mla_o, w_sb_o, w_out,
              ffn_norm, w_up, conv_w, conv_b, w_down, final_norm):
    B, S, _ = x_prompt.shape
    T = x_sample.shape[1]
    past = page_table.shape[1] * PAGE_SIZE
    pos_p = jnp.arange(S)
    pos_s = past + jnp.arange(T)
    hp, hs = x_prompt, x_sample
    lat_p, kr_p, sbk_p, sbv_p, conv_p = [], [], [], [], []
    lat_s, kr_s, sbk_s, sbv_s, conv_s = [], [], [], [], []
    for l in range(DEPTH):
        lw = (w_in[l], q_norm[l], kv_norm[l], w_uq[l], w_ukv[l], w_mla_o[l], w_sb_o[l], w_out[l])
        y, c, kr, k, v = _attention_branches(_rmsnorm(hp, attn_norm[l]), pos_p, *lw, attend=_prompt_attention)
        hp = hp + y
        f, cs = _conv_ffn(_rmsnorm(hp, ffn_norm[l]), jnp.zeros((B, CONV_W - 1, 2 * D_FF), hp.dtype),
                          w_up[l], conv_w[l], conv_b[l], w_down[l])
        hp = hp + f
        lat_p.append(c); kr_p.append(kr); sbk_p.append(k); sbv_p.append(v); conv_p.append(cs)
        attend_s = functools.partial(_sample_attention, lat_pool=cache_latent[l], kr_pool=cache_krope[l],
                                     sbk_pool=cache_sb_k[l], sbv_pool=cache_sb_v[l], page_table=page_table)
        y, c, kr, k, v = _attention_branches(_rmsnorm(hs, attn_norm[l]), pos_s, *lw, attend=attend_s)
        hs = hs + y
        f, cs = _conv_ffn(_rmsnorm(hs, ffn_norm[l]), state_conv[l], w_up[l], conv_w[l], conv_b[l], w_down[l])
        hs = hs + f
        lat_s.append(c); kr_s.append(kr); sbk_s.append(k); sbv_s.append(v); conv_s.append(cs)
    y_prompt = _rmsnorm(hp, final_norm)
    y_sample = _rmsnorm(hs, final_norm)
    return (y_prompt, y_sample,
            jnp.stack(lat_p), jnp.stack(kr_p), jnp.stack(sbk_p), jnp.stack(sbv_p), jnp.stack(conv_p),
            jnp.stack(lat_s), jnp.stack(kr_s), jnp.stack(sbk_s), jnp.stack(sbv_s), jnp.stack(conv_s))
```

```python
import functools

import jax
import jax.numpy as jnp
from jax import lax
from jax.experimental import pallas as pl
from jax.experimental.pallas import tpu as pltpu

D_MODEL = 1024
PAGE_SIZE = 128
MLA_HEADS = 8
MLA_NOPE_DIM = 64
MLA_ROPE_DIM = 32
MLA_V_DIM = 64
MLA_Q_LORA = 256
MLA_KV_LORA = 128
ROPE_BASE = 10000.0
MLA_SCALE = (MLA_NOPE_DIM + MLA_ROPE_DIM) ** -0.5
SB_HEADS = 8
SB_HEAD_DIM = 64
SB_WIDTH = SB_HEADS * SB_HEAD_DIM
SB_SCALE = SB_HEAD_DIM ** -0.5
D_FF = 2816
CONV_W = 3
NORM_EPS = 1e-6

LANES = 128
QCAT = 2 * LANES
ROPE_HALF = MLA_ROPE_DIM // 2
MASK_VALUE = -1e30
VMEM_LIMIT = 56 * 1024 * 1024

BF16 = jnp.bfloat16
F32 = jnp.float32

_NT = (((1,), (1,)), ((), ()))


def _dot(a, b):
    return jnp.dot(a, b, preferred_element_type=F32)


def _dot_nt(a, b):
    return lax.dot_general(a, b, _NT, preferred_element_type=F32)


def _rmsnorm(x, g):
    return x * lax.rsqrt(jnp.mean(x * x, axis=-1, keepdims=True) + NORM_EPS) * g


def _sigmoid(x):
    return 1.0 / (1.0 + jnp.exp(-x))


def _rope(x, c, s1, s2):
    return x * c + pltpu.roll(x, ROPE_HALF, 1) * s1 + pltpu.roll(x, LANES - ROPE_HALF, 1) * s2


def _log_sigmoid_pair(z):
    soft = jnp.log(1.0 + jnp.exp(-jnp.abs(z)))
    ls = jnp.minimum(z, 0.0) - soft
    return ls, ls - z


def _split_bf16(x):
    hi = x.astype(BF16)
    lo = (x - hi.astype(F32)).astype(BF16)
    return hi, lo


def _proj_kernel(x_ref, g_ref, w_ref, qn_ref, kvn_ref, wuq_ref, wuk_ref, c_ref, s1_ref, s2_ref,
                 qcat_ref, kcat_ref, ckv_ref, kr_ref, sq_ref, sk_ref, sv_ref, skb_ref, svb_ref):
    xn = _rmsnorm(x_ref[...], g_ref[...]).astype(BF16)
    y = _dot(xn, w_ref[...])
    c, s1, s2 = c_ref[...], s1_ref[...], s2_ref[...]

    cqn = _rmsnorm(y[:, 0:MLA_Q_LORA], qn_ref[...]).astype(BF16)
    q2 = _dot(cqn, wuq_ref[...])
    nope_w = MLA_HEADS * MLA_NOPE_DIM
    for p in range(MLA_HEADS // 2):
        ql = _dot(q2[:, p * LANES:(p + 1) * LANES].astype(BF16), wuk_ref[p]) * MLA_SCALE
        for e in range(2):
            qcat_ref[2 * p + e, :, 0:LANES] = ql[:, e * LANES:(e + 1) * LANES].astype(BF16)
    for h in range(MLA_HEADS):
        xr = q2[:, nope_w + h * LANES:nope_w + (h + 1) * LANES]
        qcat_ref[h, :, LANES:QCAT] = (_rope(xr, c, s1, s2) * MLA_SCALE).astype(BF16)

    o = MLA_Q_LORA
    ckv = _rmsnorm(y[:, o:o + MLA_KV_LORA], kvn_ref[...])
    ckv_ref[...] = ckv
    kcat_ref[:, 0:LANES] = ckv.astype(BF16)
    o += MLA_KV_LORA
    kr = _rope(y[:, o:o + LANES], c, s1, s2)
    kr_ref[...] = kr
    kcat_ref[:, LANES:QCAT] = kr.astype(BF16)
    o += LANES
    sq_ref[...] = (y[:, o:o + SB_WIDTH] * SB_SCALE).astype(BF16)
    o += SB_WIDTH
    sk = y[:, o:o + SB_WIDTH]
    sk_ref[...] = sk
    skb_ref[...] = sk.astype(BF16)
    o += SB_WIDTH
    sv = y[:, o:o + SB_WIDTH]
    sv_ref[...] = sv
    svb_ref[...] = sv.astype(BF16)


def _project(x, tabs, n_tab_blocks, w, tm):
    m = x.shape[0]
    full = lambda a: pl.BlockSpec(a.shape, lambda i: (0,) * a.ndim)
    row = lambda n: pl.BlockSpec((tm, n), lambda i: (i, 0))
    tab = pl.BlockSpec((tm, LANES), lambda i: (i % n_tab_blocks, 0))
    out_shape = (
        jax.ShapeDtypeStruct((MLA_HEADS, m, QCAT), BF16),
        jax.ShapeDtypeStruct((m, QCAT), BF16),
        jax.ShapeDtypeStruct((m, MLA_KV_LORA), F32),
        jax.ShapeDtypeStruct((m, LANES), F32),
        jax.ShapeDtypeStruct((m, SB_WIDTH), BF16),
        jax.ShapeDtypeStruct((m, SB_WIDTH), F32),
        jax.ShapeDtypeStruct((m, SB_WIDTH), F32),
        jax.ShapeDtypeStruct((m, SB_WIDTH), BF16),
        jax.ShapeDtypeStruct((m, SB_WIDTH), BF16),
    )
    out_specs = (
        pl.BlockSpec((MLA_HEADS, tm, QCAT), lambda i: (0, i, 0)),
        row(QCAT), row(MLA_KV_LORA), row(LANES), row(SB_WIDTH), row(SB_WIDTH), row(SB_WIDTH),
        row(SB_WIDTH), row(SB_WIDTH),
    )
    ins = (x, w["attn_norm"], w["w_in_p"], w["q_norm"], w["kv_norm"], w["w_uq2"], w["w_uk_bd"]) + tabs
    in_specs = [row(D_MODEL)] + [full(a) for a in ins[1:7]] + [tab, tab, tab]
    return pl.pallas_call(
        _proj_kernel, grid=(m // tm,), in_specs=in_specs, out_specs=out_specs, out_shape=out_shape,
        compiler_params=pltpu.CompilerParams(dimension_semantics=("arbitrary",), vmem_limit_bytes=VMEM_LIMIT),
        name="proj",
    )(*ins)


def _mla_kernel(q_ref, k_ref, o_ref, m_s, l_s, acc_s, *, tq, tk):
    i = pl.program_id(1)
    rows = MLA_HEADS * tq
    q = q_ref[...].reshape(rows, QCAT)
    m_s[...] = jnp.full((rows, LANES), MASK_VALUE, F32)
    l_s[...] = jnp.zeros((rows, LANES), F32)
    acc_s[...] = jnp.zeros((rows, LANES), F32)

    def block(j, masked):
        k = k_ref[pl.ds(pl.multiple_of(j * tk, tk), tk), :]
        s = _dot_nt(q, k)
        if masked:
            qpos = i * tq + jnp.bitwise_and(lax.broadcasted_iota(jnp.int32, (rows, tk), 0), tq - 1)
            kpos = j * tk + lax.broadcasted_iota(jnp.int32, (rows, tk), 1)
            s = jnp.where(kpos <= qpos, s, MASK_VALUE)
        m_prev = m_s[...]
        m_next = jnp.maximum(m_prev, jnp.max(s, axis=1, keepdims=True))
        alpha = jnp.exp(m_prev - m_next)
        p = jnp.concatenate([jnp.exp(s[:, c * LANES:(c + 1) * LANES] - m_next) for c in range(tk // LANES)], axis=1)
        l_s[...] = alpha * l_s[...] + jnp.sum(p, axis=1, keepdims=True)
        acc_s[...] = alpha * acc_s[...] + _dot(p.astype(BF16), k[:, 0:MLA_KV_LORA])
        m_s[...] = m_next

    n_full = (i * tq) // tk

    def body(j, carry):
        block(j, False)
        return carry

    lax.fori_loop(0, n_full, body, 0)
    block(n_full, True)
    o = acc_s[...] / l_s[...]
    for h in range(MLA_HEADS):
        o_ref[:, h * LANES:(h + 1) * LANES] = o[h * tq:(h + 1) * tq].astype(o_ref.dtype)


def _mla_prompt(qcat, kcat, batch, seq, tq=128, tk=256):
    assert tk % tq == 0 and seq % tk == 0 and tq & (tq - 1) == 0
    m = batch * seq
    nq = seq // tq
    rows = MLA_HEADS * tq
    return pl.pallas_call(
        functools.partial(_mla_kernel, tq=tq, tk=tk),
        grid=(batch, nq),
        in_specs=[pl.BlockSpec((MLA_HEADS, tq, QCAT), lambda b, i: (0, b * nq + i, 0)),
                  pl.BlockSpec((seq, QCAT), lambda b, i: (b, 0))],
        out_specs=pl.BlockSpec((tq, MLA_HEADS * MLA_KV_LORA), lambda b, i: (b * nq + i, 0)),
        out_shape=jax.ShapeDtypeStruct((m, MLA_HEADS * MLA_KV_LORA), BF16),
        scratch_shapes=[pltpu.VMEM((rows, LANES), F32)] * 3,
        compiler_params=pltpu.CompilerParams(dimension_semantics=("arbitrary", "arbitrary"),
                                             vmem_limit_bytes=VMEM_LIMIT),
        name="mla_prompt",
    )(qcat, kcat)


def _sb_kernel(q_ref, k_ref, v_ref, t_ref, o_ref, acc_s, carry_s, *, t):
    i = pl.program_id(2)
    q = q_ref[...].astype(F32)
    tri = t_ref[...]
    lane = lax.broadcasted_iota(jnp.int32, (t, LANES), 1)
    row = lax.broadcasted_iota(jnp.int32, (t, t), 0)
    col = lax.broadcasted_iota(jnp.int32, (t, t), 1)
    strict = col < row
    out = jnp.zeros((t, LANES), F32)
    for e in range(2):
        head_lanes = (lane >= SB_HEAD_DIM) if e else (lane < SB_HEAD_DIM)
        qh = jnp.where(head_lanes, q, 0.0).astype(BF16)
        acc_s[...] = jnp.zeros((t, LANES), F32)
        carry_s[...] = jnp.zeros((t, LANES), F32)

        def block(j, masked):
            start = pl.multiple_of(j * t, t)
            k = k_ref[pl.ds(start, t), :]
            v = v_ref[pl.ds(start, t), :]
            z = _dot_nt(qh, k)
            ls, lk = _log_sigmoid_pair(z)
            if masked:
                lk = jnp.where(strict, lk, 0.0)
            hi, lo = _split_bf16(lk)
            after = _dot(hi, tri) + _dot(lo, tri)
            carry = carry_s[...]
            log_a = jnp.concatenate(
                [ls[:, c * LANES:(c + 1) * LANES] + after[:, c * LANES:(c + 1) * LANES] + carry
                 for c in range(t // LANES)], axis=1)
            if masked:
                log_a = jnp.where(strict, log_a, MASK_VALUE)
            a = jnp.exp(log_a)
            acc_s[...] += _dot(a.astype(BF16), v)
            carry_s[...] = carry + jnp.sum(lk, axis=1, keepdims=True)

        block(i, True)

        def body(n, c):
            block(i - 1 - n, False)
            return c

        lax.fori_loop(0, i, body, 0)
        out = jnp.where(head_lanes, acc_s[...], out)
    o_ref[...] = out.astype(o_ref.dtype)


def _sb_prompt(sq, skb, svb, tri, batch, seq, t=256):
    assert seq % t == 0
    m = batch * seq
    nq = seq // t
    pairs = SB_HEADS // 2
    return pl.pallas_call(
        functools.partial(_sb_kernel, t=t),
        grid=(batch, pairs, nq),
        in_specs=[pl.BlockSpec((t, LANES), lambda b, p, i: (b * nq + i, p)),
                  pl.BlockSpec((seq, LANES), lambda b, p, i: (b, p)),
                  pl.BlockSpec((seq, LANES), lambda b, p, i: (b, p)),
                  pl.BlockSpec((t, t), lambda b, p, i: (0, 0))],
        out_specs=pl.BlockSpec((t, LANES), lambda b, p, i: (b * nq + i, p)),
        out_shape=jax.ShapeDtypeStruct((m, SB_WIDTH), BF16),
        scratch_shapes=[pltpu.VMEM((t, LANES), F32)] * 2,
        compiler_params=pltpu.CompilerParams(dimension_semantics=("arbitrary",) * 3,
                                             vmem_limit_bytes=VMEM_LIMIT),
        name="sb_prompt",
    )(sq, skb, svb, tri)


def _sample_attn_kernel(pt_ref, qcat_ref, knew_ref, sq_ref, t_ref, *rest, group):
    del pt_ref
    lat_refs = rest[0:group]
    kr_refs = rest[group:2 * group]
    sbk_refs = rest[2 * group:3 * group]
    sbv_refs = rest[3 * group:4 * group]
    olat_ref, osb_ref, m_s, l_s, acc_s, carry_s, sbacc_s = rest[4 * group:]
    g = pl.program_id(1)
    qc = qcat_ref[...]

    @pl.when(g == 0)
    def _():
        kn = knew_ref[...].astype(F32)
        s_self = jnp.sum(qc.astype(F32) * kn, axis=1, keepdims=True)
        m_s[...] = jnp.broadcast_to(s_self, (MLA_HEADS, LANES))
        l_s[...] = jnp.ones((MLA_HEADS, LANES), F32)
        acc_s[...] = jnp.broadcast_to(kn[:, 0:MLA_KV_LORA], (MLA_HEADS, MLA_KV_LORA))
        carry_s[...] = jnp.zeros((SB_HEADS, LANES), F32)
        sbacc_s[...] = jnp.zeros((SB_HEADS, SB_WIDTH), F32)

    q_lat = qc[:, 0:LANES]
    q_rope = qc[:, LANES:LANES + MLA_ROPE_DIM]
    head_of_lane = lax.broadcasted_iota(jnp.int32, (SB_HEADS, SB_WIDTH), 1) // SB_HEAD_DIM
    own_head = head_of_lane == lax.broadcasted_iota(jnp.int32, (SB_HEADS, SB_WIDTH), 0)
    sq = jnp.broadcast_to(sq_ref[...].astype(F32), (SB_HEADS, SB_WIDTH))
    q_sb = jnp.where(own_head, sq, 0.0).astype(BF16)
    tri = t_ref[...]

    m, l, acc = m_s[...], l_s[...], acc_s[...]
    carry, sbacc = carry_s[...], sbacc_s[...]
    for r in range(group):
        lat = lat_refs[r][...].astype(BF16)
        kr_t = kr_refs[r][...].astype(BF16)
        s = _dot_nt(q_lat, lat) + _dot(q_rope, kr_t)
        m_next = jnp.maximum(m, jnp.max(s, axis=1, keepdims=True))
        alpha = jnp.exp(m - m_next)
        p = jnp.exp(s - m_next)
        l = alpha * l + jnp.sum(p, axis=1, keepdims=True)
        acc = alpha * acc + _dot(p.astype(BF16), lat)
        m = m_next

        k_t = sbk_refs[r][...].astype(BF16)
        v_t = sbv_refs[r][...].astype(BF16)
        z = _dot(q_sb, k_t)
        ls, lk = _log_sigmoid_pair(z)
        hi, lo = _split_bf16(lk)
        after = _dot(hi, tri) + _dot(lo, tri) + carry
        a = jnp.exp(ls + after)
        sbacc = sbacc + _dot_nt(a.astype(BF16), v_t)
        carry = carry + jnp.sum(lk, axis=1, keepdims=True)
    m_s[...], l_s[...], acc_s[...] = m, l, acc
    carry_s[...], sbacc_s[...] = carry, sbacc

    @pl.when(g == pl.num_programs(1) - 1)
    def _():
        olat_ref[...] = acc / l
        osb_ref[...] = jnp.sum(jnp.where(own_head, sbacc, 0.0), axis=0, keepdims=True)


def _sample_attention(qcat_s, knew, sq_s, tri, lat_pool, kr_pool, sbk_pool, sbv_pool, page_table, group=8):
    db, n_pages = page_table.shape
    assert n_pages % group == 0
    steps = n_pages // group

    def page_spec(pool, r):
        return pl.BlockSpec((None,) + pool.shape[1:],
                            lambda b, g, pt: (pt[b, n_pages - 1 - (g * group + r)], 0, 0))

    per_b = lambda shape: pl.BlockSpec((None,) + shape, lambda b, g, pt: (b, 0, 0))
    in_specs = [per_b((MLA_HEADS, QCAT)), per_b((1, QCAT)), per_b((1, SB_WIDTH)),
                pl.BlockSpec((PAGE_SIZE, PAGE_SIZE), lambda b, g, pt: (0, 0))]
    operands = [qcat_s, knew, sq_s, tri]
    for pool in (lat_pool, kr_pool, sbk_pool, sbv_pool):
        for r in range(group):
            in_specs.append(page_spec(pool, r))
            operands.append(pool)
    grid_spec = pltpu.PrefetchScalarGridSpec(
        num_scalar_prefetch=1, grid=(db, steps), in_specs=in_specs,
        out_specs=(per_b((MLA_HEADS, MLA_KV_LORA)), per_b((1, SB_WIDTH))),
        scratch_shapes=[pltpu.VMEM((MLA_HEADS, LANES), F32)] * 4 + [pltpu.VMEM((SB_HEADS, SB_WIDTH), F32)])
    return pl.pallas_call(
        functools.partial(_sample_attn_kernel, group=group),
        grid_spec=grid_spec,
        out_shape=(jax.ShapeDtypeStruct((db, MLA_HEADS, MLA_KV_LORA), F32),
                   jax.ShapeDtypeStruct((db, 1, SB_WIDTH), F32)),
        compiler_params=pltpu.CompilerParams(dimension_semantics=("arbitrary", "arbitrary"),
                                             vmem_limit_bytes=VMEM_LIMIT),
        name="sample_attn",
    )(page_table, *operands)


def _mix_kernel(x_ref, g_ref, wg_ref, olat_ref, wuv_ref, wmo_ref, osb_ref, wso_ref, wout_ref, g2_ref,
                h_ref, hn_ref):
    x = x_ref[...]
    xn = _rmsnorm(x, g_ref[...]).astype(BF16)
    gates = _dot(xn, wg_ref[...])
    olat = olat_ref[...].astype(BF16)
    om = jnp.concatenate([_dot(olat[:, p * QCAT:(p + 1) * QCAT], wuv_ref[p]) for p in range(MLA_HEADS // 2)],
                         axis=1).astype(BF16)
    o_mla = _dot(om, wmo_ref[...])
    o_sb = _dot(osb_ref[...].astype(BF16), wso_ref[...])
    mix = _sigmoid(gates[:, 0:D_MODEL]) * o_mla + _sigmoid(gates[:, D_MODEL:2 * D_MODEL]) * o_sb
    h = x + _dot(mix.astype(BF16), wout_ref[...])
    h_ref[...] = h
    hn_ref[...] = _rmsnorm(h, g2_ref[...]).astype(BF16)


def _mix(x, olat, osb, w, tm):
    m = x.shape[0]
    full = lambda a: pl.BlockSpec(a.shape, lambda i: (0,) * a.ndim)
    row = lambda n: pl.BlockSpec((tm, n), lambda i: (i, 0))
    ins = (x, w["attn_norm"], w["w_gate"], olat, w["w_uv_bd"], w["w_mla_o"], osb, w["w_sb_o"], w["w_out"],
           w["ffn_norm"])
    in_specs = [row(D_MODEL), full(ins[1]), full(ins[2]), row(olat.shape[1]), full(ins[4]), full(ins[5]),
                row(SB_WIDTH), full(ins[7]), full(ins[8]), full(ins[9])]
    return pl.pallas_call(
        _mix_kernel, grid=(m // tm,), in_specs=in_specs,
        out_specs=(row(D_MODEL), row(D_MODEL)),
        out_shape=(jax.ShapeDtypeStruct((m, D_MODEL), F32), jax.ShapeDtypeStruct((m, D_MODEL), BF16)),
        compiler_params=pltpu.CompilerParams(dimension_semantics=("arbitrary",), vmem_limit_bytes=VMEM_LIMIT),
        name="mix",
    )(*ins)


FFN_CHUNK = 256


SUBLANES = 8


def _ffn_chunks(hn, wup_ref, cw_ref, cb_ref, wdn_ref, shifted):
    acc = jnp.zeros((hn.shape[0], D_MODEL), F32)
    for c in range(D_FF // FFN_CHUNK):
        conv = []
        for half in range(2):
            lo = half * D_FF + c * FFN_CHUNK
            cols = slice(lo, lo + FFN_CHUNK)
            u = _dot(hn, wup_ref[:, cols])
            u1, u2 = shifted(u, cols)
            conv.append(cb_ref[:, cols] + cw_ref[0:1, cols] * u2 + cw_ref[1:2, cols] * u1 + cw_ref[2:3, cols] * u)
        a, b = conv
        gated = (a * _sigmoid(a) * b).astype(BF16)
        acc = acc + _dot(gated, wdn_ref[c * FFN_CHUNK:(c + 1) * FFN_CHUNK, :])
    return acc


def _ffn_out(h_ref, acc, gf_ref, y_ref, final):
    y = h_ref[...] + acc
    y_ref[...] = _rmsnorm(y, gf_ref[...]) if final else y


def _ffn_seq_kernel(hn_ref, h_ref, wup_ref, cw_ref, cb_ref, wdn_ref, gf_ref, y_ref, st_ref, tail_s, *,
                    tm, tiles_per_seq, final):
    @pl.when(pl.program_id(0) % tiles_per_seq == 0)
    def _():
        tail_s[...] = jnp.zeros(tail_s.shape, F32)

    rowid = lax.broadcasted_iota(jnp.int32, (tm, FFN_CHUNK), 0)

    def shifted(u, cols):
        p0, p1 = tail_s[SUBLANES - 2:SUBLANES - 1, cols], tail_s[SUBLANES - 1:SUBLANES, cols]
        u1 = jnp.where(rowid == 0, p1, pltpu.roll(u, 1, 0))
        u2 = jnp.where(rowid == 0, p0, jnp.where(rowid == 1, p1, pltpu.roll(u, 2, 0)))
        tail = u[tm - SUBLANES:tm]
        tail_s[:, cols] = tail
        st_ref[0, :, cols] = tail
        return u1, u2

    acc = _ffn_chunks(hn_ref[...], wup_ref, cw_ref, cb_ref, wdn_ref, shifted)
    _ffn_out(h_ref, acc, gf_ref, y_ref, final)


def _ffn_step_kernel(hn_ref, h_ref, wup_ref, cw_ref, cb_ref, wdn_ref, gf_ref, prev_ref, y_ref, st_ref, *, final):
    def shifted(u, cols):
        later = slice(2 * D_FF + cols.start, 2 * D_FF + cols.stop)
        p0, p1 = prev_ref[:, cols], prev_ref[:, later]
        st_ref[:, cols] = p1
        st_ref[:, later] = u
        return p1, p0

    acc = _ffn_chunks(hn_ref[...], wup_ref, cw_ref, cb_ref, wdn_ref, shifted)
    _ffn_out(h_ref, acc, gf_ref, y_ref, final)


def _ffn(hn, h, prev, w, tm, seq, final):
    m = hn.shape[0]
    full = lambda a: pl.BlockSpec(a.shape, lambda i: (0,) * a.ndim)
    row = lambda n: pl.BlockSpec((tm, n), lambda i: (i, 0))
    weights = (w["w_up"], w["conv_w"], w["conv_b"], w["w_down"], w["final_norm"])
    in_specs = [row(D_MODEL), row(D_MODEL)] + [full(a) for a in weights]
    params = pltpu.CompilerParams(dimension_semantics=("arbitrary",), vmem_limit_bytes=VMEM_LIMIT)
    y_shape = jax.ShapeDtypeStruct((m, D_MODEL), F32)
    if prev is None:
        assert seq % tm == 0 and tm >= SUBLANES
        tiles_per_seq = seq // tm
        y, st = pl.pallas_call(
            functools.partial(_ffn_seq_kernel, tm=tm, tiles_per_seq=tiles_per_seq, final=final),
            grid=(m // tm,), in_specs=in_specs,
            out_specs=(row(D_MODEL), pl.BlockSpec((1, SUBLANES, 2 * D_FF), lambda i: (i // tiles_per_seq, 0, 0))),
            out_shape=(y_shape, jax.ShapeDtypeStruct((m // seq, SUBLANES, 2 * D_FF), F32)),
            scratch_shapes=[pltpu.VMEM((SUBLANES, 2 * D_FF), F32)],
            compiler_params=params, name="ffn_seq",
        )(hn, h, *weights)
        return y, st[:, SUBLANES - (CONV_W - 1):]
    assert seq == 1
    st_w = (CONV_W - 1) * 2 * D_FF
    y, st = pl.pallas_call(
        functools.partial(_ffn_step_kernel, final=final),
        grid=(m // tm,), in_specs=in_specs + [row(st_w)],
        out_specs=(row(D_MODEL), row(st_w)),
        out_shape=(y_shape, jax.ShapeDtypeStruct((m, st_w), F32)),
        compiler_params=params, name="ffn_step",
    )(hn, h, *weights, prev.reshape(m, st_w))
    return y, st.reshape(m, CONV_W - 1, 2 * D_FF)


def _prep_weights(l, attn_norm, w_in, q_norm, kv_norm, w_uq, w_ukv, w_mla_o, w_sb_o, w_out, ffn_norm, w_up, conv_w,
                  conv_b, w_down, final_norm):
    w_in = w_in[l]
    o_kr = MLA_Q_LORA + MLA_KV_LORA
    o_sq = o_kr + MLA_ROPE_DIM
    o_g = o_sq + 3 * SB_WIDTH
    pad = jnp.zeros((D_MODEL, LANES - MLA_ROPE_DIM), w_in.dtype)
    w_in_p = jnp.concatenate([w_in[:, :o_sq], pad, w_in[:, o_sq:o_g]], axis=1).astype(BF16)
    w_gate = w_in[:, o_g:].astype(BF16)
    per_head = MLA_NOPE_DIM + MLA_ROPE_DIM
    uq3 = w_uq[l].reshape(MLA_Q_LORA, MLA_HEADS, per_head)
    uq_nope = uq3[:, :, :MLA_NOPE_DIM].reshape(MLA_Q_LORA, MLA_HEADS * MLA_NOPE_DIM)
    uq_rope = jnp.pad(uq3[:, :, MLA_NOPE_DIM:], ((0, 0), (0, 0), (0, LANES - MLA_ROPE_DIM)))
    w_uq2 = jnp.concatenate([uq_nope, uq_rope.reshape(MLA_Q_LORA, MLA_HEADS * LANES)], axis=1).astype(BF16)
    ukv3 = w_ukv[l].reshape(MLA_KV_LORA, MLA_HEADS, MLA_NOPE_DIM + MLA_V_DIM)
    uk_t = jnp.transpose(ukv3[:, :, :MLA_NOPE_DIM], (1, 2, 0))
    uv = jnp.transpose(ukv3[:, :, MLA_NOPE_DIM:], (1, 0, 2))
    zk = jnp.zeros_like(uk_t[0])
    zv = jnp.zeros_like(uv[0])
    w_uk_bd = jnp.stack([jnp.block([[uk_t[2 * p], zk], [zk, uk_t[2 * p + 1]]]) for p in range(MLA_HEADS // 2)])
    w_uv_bd = jnp.stack([jnp.block([[uv[2 * p], zv], [zv, uv[2 * p + 1]]]) for p in range(MLA_HEADS // 2)])
    return {
        "attn_norm": attn_norm[l][None, :], "w_in_p": w_in_p, "w_gate": w_gate,
        "q_norm": q_norm[l][None, :], "kv_norm": kv_norm[l][None, :], "w_uq2": w_uq2,
        "w_uk_bd": w_uk_bd.astype(BF16), "w_uv_bd": w_uv_bd.astype(BF16),
        "w_mla_o": w_mla_o[l].astype(BF16), "w_sb_o": w_sb_o[l].astype(BF16), "w_out": w_out[l].astype(BF16),
        "ffn_norm": ffn_norm[l][None, :], "w_up": w_up[l].astype(BF16), "conv_w": conv_w[l],
        "conv_b": conv_b[l][None, :], "w_down": w_down[l].astype(BF16), "final_norm": final_norm[None, :],
    }


def _rope_tables(pos):
    inv = ROPE_BASE ** (-jnp.arange(ROPE_HALF, dtype=F32) / ROPE_HALF)
    ang = pos.astype(F32)[:, None] * inv[None, :]
    cos, sin = jnp.cos(ang), jnp.sin(ang)
    z = jnp.zeros_like(cos)
    zpad = jnp.zeros((pos.shape[0], LANES - MLA_ROPE_DIM), F32)
    return (jnp.concatenate([cos, cos, zpad], axis=1), jnp.concatenate([z, sin, zpad], axis=1),
            jnp.concatenate([-sin, z, zpad], axis=1))


def _tri(n):
    r = lax.broadcasted_iota(jnp.int32, (n, n), 0)
    c = lax.broadcasted_iota(jnp.int32, (n, n), 1)
    return (r > c).astype(BF16)


def _pick_tile(n, target):
    t = min(n, target)
    assert n % t == 0
    return t


def kernel(x_prompt, x_sample, cache_latent, cache_krope, cache_sb_k, cache_sb_v, state_conv, page_table, attn_norm, w_in, q_norm, kv_norm, w_uq, w_ukv, w_mla_o, w_sb_o, w_out, ffn_norm, w_up, conv_w, conv_b, w_down, final_norm):
    batch, seq, _ = x_prompt.shape
    db, dec_seq, _ = x_sample.shape
    assert dec_seq == 1
    depth = w_in.shape[0]
    n_pool = cache_latent.shape[1]
    n_pages = page_table.shape[1]
    mp, ms = batch * seq, db * dec_seq
    tabs_p = _rope_tables(jnp.arange(seq))
    tabs_s = _rope_tables(jnp.full((ms,), n_pages * PAGE_SIZE, jnp.int32))
    sb_t = _pick_tile(seq, 256)
    tri_p, tri_s = _tri(sb_t), _tri(PAGE_SIZE)
    tm_p = _pick_tile(seq, 512)
    tm_f = _pick_tile(seq, 256)
    tm_s = _pick_tile(ms, 128)

    def keys_minor(pool):
        return jnp.transpose(pool, (0, 2, 3, 1)).reshape(n_pool, SB_WIDTH, PAGE_SIZE)

    hp = x_prompt.reshape(mp, D_MODEL)
    hs = x_sample.reshape(ms, D_MODEL)
    outs = [[] for _ in range(10)]
    for l in range(depth):
        final = l == depth - 1
        w = _prep_weights(l, attn_norm, w_in, q_norm, kv_norm, w_uq, w_ukv, w_mla_o, w_sb_o, w_out, ffn_norm, w_up,
                          conv_w, conv_b, w_down, final_norm)
        qcat, kcat, ckv, kr, sq, sk, sv, skb, svb = _project(hp, tabs_p, seq // tm_p, w, tm_p)
        olat = _mla_prompt(qcat, kcat, batch, seq, tq=_pick_tile(seq, 128), tk=_pick_tile(seq, 256))
        osb = _sb_prompt(sq, skb, svb, tri_p, batch, seq, t=sb_t)
        h, hn = _mix(hp, olat, osb, w, tm_p)
        hp, conv_st = _ffn(hn, h, None, w, tm_f, seq, final)
        outs[0].append(ckv.reshape(batch, seq, MLA_KV_LORA))
        outs[1].append(kr[:, :MLA_ROPE_DIM].reshape(batch, seq, MLA_ROPE_DIM))
        outs[2].append(sk.reshape(batch, seq, SB_HEADS, SB_HEAD_DIM))
        outs[3].append(sv.reshape(batch, seq, SB_HEADS, SB_HEAD_DIM))
        outs[4].append(conv_st)
        qcat, kcat, ckv, kr, sq, sk, sv, _, _ = _project(hs, tabs_s, 1, w, tm_s)
        olat, osb = _sample_attention(
            jnp.swapaxes(qcat, 0, 1), jnp.concatenate([ckv, kr], axis=1).reshape(ms, 1, QCAT),
            sq.reshape(ms, 1, SB_WIDTH), tri_s,
            cache_latent[l], jnp.swapaxes(cache_krope[l], 1, 2), keys_minor(cache_sb_k[l]), keys_minor(cache_sb_v[l]),
            page_table)
        h, hn = _mix(hs, olat.reshape(ms, MLA_HEADS * MLA_KV_LORA), osb.reshape(ms, SB_WIDTH), w, tm_s)
        hs, conv_st = _ffn(hn, h, state_conv[l], w, tm_s, 1, final)
        outs[5].append(ckv.reshape(db, dec_seq, MLA_KV_LORA))
        outs[6].append(kr[:, :MLA_ROPE_DIM].reshape(db, dec_seq, MLA_ROPE_DIM))
        outs[7].append(sk.reshape(db, dec_seq, SB_HEADS, SB_HEAD_DIM))
        outs[8].append(sv.reshape(db, dec_seq, SB_HEADS, SB_HEAD_DIM))
        outs[9].append(conv_st)
    return (hp.reshape(batch, seq, D_MODEL), hs.reshape(db, dec_seq, D_MODEL)) + tuple(jnp.stack(o) for o in outs)
```

```python
import functools

import jax
import jax.numpy as jnp
from jax import lax
from jax.experimental import pallas as pl
from jax.experimental.pallas import tpu as pltpu

D_MODEL = 1024
PAGE_SIZE = 128
MLA_HEADS = 8
MLA_NOPE_DIM = 64
MLA_ROPE_DIM = 32
MLA_V_DIM = 64
MLA_Q_LORA = 256
MLA_KV_LORA = 128
ROPE_BASE = 10000.0
MLA_SCALE = (MLA_NOPE_DIM + MLA_ROPE_DIM) ** -0.5
SB_HEADS = 8
SB_HEAD_DIM = 64
SB_WIDTH = SB_HEADS * SB_HEAD_DIM
SB_SCALE = SB_HEAD_DIM ** -0.5
D_FF = 2816
CONV_W = 3
NORM_EPS = 1e-6

LANES = 128
QCAT = 2 * LANES
ROPE_HALF = MLA_ROPE_DIM // 2
MASK_VALUE = -1e30
VMEM_LIMIT = 56 * 1024 * 1024

BF16 = jnp.bfloat16
F32 = jnp.float32

_NT = (((1,), (1,)), ((), ()))


def _dot(a, b):
    return jnp.dot(a, b, preferred_element_type=F32)


def _dot_nt(a, b):
    return lax.dot_general(a, b, _NT, preferred_element_type=F32)


def _rmsnorm(x, g):
    return x * lax.rsqrt(jnp.mean(x * x, axis=-1, keepdims=True) + NORM_EPS) * g


def _sigmoid(x):
    return 1.0 / (1.0 + jnp.exp(-x))


def _rope(x, c, s1, s2):
    return x * c + pltpu.roll(x, ROPE_HALF, 1) * s1 + pltpu.roll(x, LANES - ROPE_HALF, 1) * s2


def _log_sigmoid_pair(z):
    soft = jnp.log(1.0 + jnp.exp(-jnp.abs(z)))
    ls = jnp.minimum(z, 0.0) - soft
    return ls, ls - z


def _split_bf16(x):
    hi = x.astype(BF16)
    lo = (x - hi.astype(F32)).astype(BF16)
    return hi, lo


def _proj_kernel(x_ref, g_ref, w_ref, qn_ref, kvn_ref, wuq_ref, wuk_ref, c_ref, s1_ref, s2_ref,
                 qcat_ref, kcat_ref, ckv_ref, kr_ref, sq_ref, sk_ref, sv_ref, skb_ref, svb_ref):
    xn = _rmsnorm(x_ref[...], g_ref[...]).astype(BF16)
    y = _dot(xn, w_ref[...])
    c, s1, s2 = c_ref[...], s1_ref[...], s2_ref[...]

    cqn = _rmsnorm(y[:, 0:MLA_Q_LORA], qn_ref[...]).astype(BF16)
    q2 = _dot(cqn, wuq_ref[...])
    nope_w = MLA_HEADS * MLA_NOPE_DIM
    for p in range(MLA_HEADS // 2):
        ql = _dot(q2[:, p * LANES:(p + 1) * LANES].astype(BF16), wuk_ref[p]) * MLA_SCALE
        for e in range(2):
            qcat_ref[2 * p + e, :, 0:LANES] = ql[:, e * LANES:(e + 1) * LANES].astype(BF16)
    for h in range(MLA_HEADS):
        xr = q2[:, nope_w + h * LANES:nope_w + (h + 1) * LANES]
        qcat_ref[h, :, LANES:QCAT] = (_rope(xr, c, s1, s2) * MLA_SCALE).astype(BF16)

    o = MLA_Q_LORA
    ckv = _rmsnorm(y[:, o:o + MLA_KV_LORA], kvn_ref[...])
    ckv_ref[...] = ckv
    kcat_ref[:, 0:LANES] = ckv.astype(BF16)
    o += MLA_KV_LORA
    kr = _rope(y[:, o:o + LANES], c, s1, s2)
    kr_ref[...] = kr
    kcat_ref[:, LANES:QCAT] = kr.astype(BF16)
    o += LANES
    sq_ref[...] = (y[:, o:o + SB_WIDTH] * SB_SCALE).astype(BF16)
    o += SB_WIDTH
    sk = y[:, o:o + SB_WIDTH]
    sk_ref[...] = sk
    skb_ref[...] = sk.astype(BF16)
    o += SB_WIDTH
    sv = y[:, o:o + SB_WIDTH]
    sv_ref[...] = sv
    svb_ref[...] = sv.astype(BF16)


def _project(x, tabs, n_tab_blocks, w, tm):
    m = x.shape[0]
    full = lambda a: pl.BlockSpec(a.shape, lambda i: (0,) * a.ndim)
    row = lambda n: pl.BlockSpec((tm, n), lambda i: (i, 0))
    tab = pl.BlockSpec((tm, LANES), lambda i: (i % n_tab_blocks, 0))
    out_shape = (
        jax.ShapeDtypeStruct((MLA_HEADS, m, QCAT), BF16),
        jax.ShapeDtypeStruct((m, QCAT), BF16),
        jax.ShapeDtypeStruct((m, MLA_KV_LORA), F32),
        jax.ShapeDtypeStruct((m, LANES), F32),
        jax.ShapeDtypeStruct((m, SB_WIDTH), BF16),
        jax.ShapeDtypeStruct((m, SB_WIDTH), F32),
        jax.ShapeDtypeStruct((m, SB_WIDTH), F32),
        jax.ShapeDtypeStruct((m, SB_WIDTH), BF16),
        jax.ShapeDtypeStruct((m, SB_WIDTH), BF16),
    )
    out_specs = (
        pl.BlockSpec((MLA_HEADS, tm, QCAT), lambda i: (0, i, 0)),
        row(QCAT), row(MLA_KV_LORA), row(LANES), row(SB_WIDTH), row(SB_WIDTH), row(SB_WIDTH),
        row(SB_WIDTH), row(SB_WIDTH),
    )
    ins = (x, w["attn_norm"], w["w_in_p"], w["q_norm"], w["kv_norm"], w["w_uq2"], w["w_uk_bd"]) + tabs
    in_specs = [row(D_MODEL)] + [full(a) for a in ins[1:7]] + [tab, tab, tab]
    return pl.pallas_call(
        _proj_kernel, grid=(m // tm,), in_specs=in_specs, out_specs=out_specs, out_shape=out_shape,
        compiler_params=pltpu.CompilerParams(dimension_semantics=("arbitrary",), vmem_limit_bytes=VMEM_LIMIT),
        name="proj",
    )(*ins)


def _mla_kernel(q_ref, k_ref, o_ref, m_s, l_s, acc_s, *, tq, tk):
    i = pl.program_id(1)
    rows = MLA_HEADS * tq
    q = q_ref[...].reshape(rows, QCAT)
    m_s[...] = jnp.full((rows, LANES), MASK_VALUE, F32)
    l_s[...] = jnp.zeros((rows, LANES), F32)
    acc_s[...] = jnp.zeros((rows, LANES), F32)

    def block(j, masked):
        k = k_ref[pl.ds(pl.multiple_of(j * tk, tk), tk), :]
        s = _dot_nt(q, k)
        if masked:
            qpos = i * tq + jnp.bitwise_and(lax.broadcasted_iota(jnp.int32, (rows, tk), 0), tq - 1)
            kpos = j * tk + lax.broadcasted_iota(jnp.int32, (rows, tk), 1)
            s = jnp.where(kpos <= qpos, s, MASK_VALUE)
        m_prev = m_s[...]
        m_next = jnp.maximum(m_prev, jnp.max(s, axis=1, keepdims=True))
        alpha = jnp.exp(m_prev - m_next)
        p = jnp.concatenate([jnp.exp(s[:, c * LANES:(c + 1) * LANES] - m_next) for c in range(tk // LANES)], axis=1)
        l_s[...] = alpha * l_s[...] + jnp.sum(p, axis=1, keepdims=True)
        acc_s[...] = alpha * acc_s[...] + _dot(p.astype(BF16), k[:, 0:MLA_KV_LORA])
        m_s[...] = m_next

    n_full = (i * tq) // tk

    def body(j, carry):
        block(j, False)
        return carry

    lax.fori_loop(0, n_full, body, 0)
    block(n_full, True)
    o = acc_s[...] / l_s[...]
    for h in range(MLA_HEADS):
        o_ref[:, h * LANES:(h + 1) * LANES] = o[h * tq:(h + 1) * tq].astype(o_ref.dtype)


def _mla_prompt(qcat, kcat, batch, seq, tq=128, tk=256):
    assert tk % tq == 0 and seq % tk == 0 and tq & (tq - 1) == 0
    m = batch * seq
    nq = seq // tq
    rows = MLA_HEADS * tq
    return pl.pallas_call(
        functools.partial(_mla_kernel, tq=tq, tk=tk),
        grid=(batch, nq),
        in_specs=[pl.BlockSpec((MLA_HEADS, tq, QCAT), lambda b, i: (0, b * nq + i, 0)),
                  pl.BlockSpec((seq, QCAT), lambda b, i: (b, 0))],
        out_specs=pl.BlockSpec((tq, MLA_HEADS * MLA_KV_LORA), lambda b, i: (b * nq + i, 0)),
        out_shape=jax.ShapeDtypeStruct((m, MLA_HEADS * MLA_KV_LORA), BF16),
        scratch_shapes=[pltpu.VMEM((rows, LANES), F32)] * 3,
        compiler_params=pltpu.CompilerParams(dimension_semantics=("arbitrary", "arbitrary"),
                                             vmem_limit_bytes=VMEM_LIMIT),
        name="mla_prompt",
    )(qcat, kcat)


def _sb_kernel(q_ref, k_ref, v_ref, t_ref, o_ref, q2_s, acc_s, carry_s, *, t, pairs):
    i = pl.program_id(2)
    tri = t_ref[...]
    lane = lax.broadcasted_iota(jnp.int32, (t, LANES), 1)
    first = lane < SB_HEAD_DIM
    row = jnp.bitwise_and(lax.broadcasted_iota(jnp.int32, (2 * t, t), 0), t - 1)
    strict = lax.broadcasted_iota(jnp.int32, (2 * t, t), 1) < row
    for p in range(pairs):
        q = q_ref[:, p * LANES:(p + 1) * LANES].astype(F32)
        q2_s[p, 0:t, :] = jnp.where(first, q, 0.0).astype(BF16)
        q2_s[p, t:2 * t, :] = jnp.where(first, 0.0, q).astype(BF16)
    acc_s[...] = jnp.zeros(acc_s.shape, F32)
    carry_s[...] = jnp.zeros(carry_s.shape, F32)

    def block(j, masked):
        start = pl.multiple_of(j * t, t)
        for p in range(pairs):
            k = k_ref[pl.ds(start, t), p * LANES:(p + 1) * LANES]
            v = v_ref[pl.ds(start, t), p * LANES:(p + 1) * LANES]
            z = _dot_nt(q2_s[p], k)
            ls, lk = _log_sigmoid_pair(z)
            if masked:
                lk = jnp.where(strict, lk, 0.0)
            hi, lo = _split_bf16(lk)
            after = _dot(hi, tri) + _dot(lo, tri)
            carry = carry_s[p]
            log_a = jnp.concatenate(
                [ls[:, c * LANES:(c + 1) * LANES] + after[:, c * LANES:(c + 1) * LANES] + carry
                 for c in range(t // LANES)], axis=1)
            if masked:
                log_a = jnp.where(strict, log_a, MASK_VALUE)
            acc_s[p] += _dot(jnp.exp(log_a).astype(BF16), v)
            carry_s[p] = carry + jnp.sum(lk, axis=1, keepdims=True)

    block(i, True)

    def body(n, c):
        block(i - 1 - n, False)
        return c

    lax.fori_loop(0, i, body, 0)
    for p in range(pairs):
        o_ref[:, p * LANES:(p + 1) * LANES] = jnp.where(first, acc_s[p, 0:t, :], acc_s[p, t:2 * t, :]).astype(o_ref.dtype)


def _sb_prompt(sq, skb, svb, tri, batch, seq, t=256, pairs=4):
    assert seq % t == 0 and t & (t - 1) == 0 and (SB_HEADS // 2) % pairs == 0
    m = batch * seq
    nq = seq // t
    w = pairs * LANES
    return pl.pallas_call(
        functools.partial(_sb_kernel, t=t, pairs=pairs),
        grid=(batch, SB_HEADS // 2 // pairs, nq),
        in_specs=[pl.BlockSpec((t, w), lambda b, p, i: (b * nq + i, p)),
                  pl.BlockSpec((seq, w), lambda b, p, i: (b, p)),
                  pl.BlockSpec((seq, w), lambda b, p, i: (b, p)),
                  pl.BlockSpec((t, t), lambda b, p, i: (0, 0))],
        out_specs=pl.BlockSpec((t, w), lambda b, p, i: (b * nq + i, p)),
        out_shape=jax.ShapeDtypeStruct((m, SB_WIDTH), BF16),
        scratch_shapes=[pltpu.VMEM((pairs, 2 * t, LANES), BF16), pltpu.VMEM((pairs, 2 * t, LANES), F32),
                        pltpu.VMEM((pairs, 2 * t, LANES), F32)],
        compiler_params=pltpu.CompilerParams(dimension_semantics=("arbitrary",) * 3,
                                             vmem_limit_bytes=VMEM_LIMIT),
        name="sb_prompt",
    )(sq, skb, svb, tri)


def _sample_attn_kernel(pt_ref, qcat_ref, knew_ref, sq_ref, t_ref, *rest, group):
    del pt_ref
    lat_refs = rest[0:group]
    kr_refs = rest[group:2 * group]
    sbk_refs = rest[2 * group:3 * group]
    sbv_refs = rest[3 * group:4 * group]
    olat_ref, osb_ref, m_s, l_s, acc_s, carry_s, sbacc_s = rest[4 * group:]
    g = pl.program_id(1)
    qc = qcat_ref[...]

    @pl.when(g == 0)
    def _():
        kn = knew_ref[...].astype(F32)
        s_self = jnp.sum(qc.astype(F32) * kn, axis=1, keepdims=True)
        m_s[...] = jnp.broadcast_to(s_self, (MLA_HEADS, LANES))
        l_s[...] = jnp.ones((MLA_HEADS, LANES), F32)
        acc_s[...] = jnp.broadcast_to(kn[:, 0:MLA_KV_LORA], (MLA_HEADS, MLA_KV_LORA))
        carry_s[...] = jnp.zeros((SB_HEADS, LANES), F32)
        sbacc_s[...] = jnp.zeros((SB_HEADS, SB_WIDTH), F32)

    q_lat = qc[:, 0:LANES]
    q_rope = qc[:, LANES:LANES + MLA_ROPE_DIM]
    head_of_lane = lax.broadcasted_iota(jnp.int32, (SB_HEADS, SB_WIDTH), 1) // SB_HEAD_DIM
    own_head = head_of_lane == lax.broadcasted_iota(jnp.int32, (SB_HEADS, SB_WIDTH), 0)
    sq = jnp.broadcast_to(sq_ref[...].astype(F32), (SB_HEADS, SB_WIDTH))
    q_sb = jnp.where(own_head, sq, 0.0).astype(BF16)
    tri = t_ref[...]

    pages = range(group)
    lat = [lat_refs[r][...].astype(BF16) for r in pages]
    s = [_dot_nt(q_lat, lat[r]) + _dot(q_rope, kr_refs[r][...].astype(BF16)) for r in pages]
    m = m_s[...]
    m_next = jnp.maximum(m, jnp.max(functools.reduce(jnp.maximum, s), axis=1, keepdims=True))
    alpha = jnp.exp(m - m_next)
    p = [jnp.exp(s[r] - m_next) for r in pages]
    l = alpha * l_s[...] + jnp.sum(functools.reduce(jnp.add, p), axis=1, keepdims=True)
    acc = alpha * acc_s[...] + functools.reduce(jnp.add, [_dot(p[r].astype(BF16), lat[r]) for r in pages])
    m_s[...], l_s[...], acc_s[...] = m_next, l, acc

    z = jnp.concatenate([_dot(q_sb, sbk_refs[r][...].astype(BF16)) for r in pages], axis=0)
    ls, lk = _log_sigmoid_pair(z)
    hi, lo = _split_bf16(lk)
    after = _dot(hi, tri) + _dot(lo, tri)
    page_sum = jnp.sum(lk, axis=1, keepdims=True)
    log_a = ls + after
    carry = carry_s[...]
    pv = []
    for r in pages:
        rows = slice(r * SB_HEADS, (r + 1) * SB_HEADS)
        a = jnp.exp(log_a[rows] + carry)
        pv.append(_dot_nt(a.astype(BF16), sbv_refs[r][...].astype(BF16)))
        carry = carry + page_sum[rows]
    sbacc = sbacc_s[...] + functools.reduce(jnp.add, pv)
    carry_s[...], sbacc_s[...] = carry, sbacc

    @pl.when(g == pl.num_programs(1) - 1)
    def _():
        olat_ref[...] = acc / l
        osb_ref[...] = jnp.sum(jnp.where(own_head, sbacc, 0.0), axis=0, keepdims=True)


def _sample_attention(qcat_s, knew, sq_s, tri, lat_pool, kr_pool, sbk_pool, sbv_pool, page_table, group=16):
    db, n_pages = page_table.shape
    assert n_pages % group == 0
    steps = n_pages // group

    def page_spec(pool, r):
        return pl.BlockSpec((None,) + pool.shape[1:],
                            lambda b, g, pt: (pt[b, n_pages - 1 - (g * group + r)], 0, 0))

    per_b = lambda shape: pl.BlockSpec((None,) + shape, lambda b, g, pt: (b, 0, 0))
    in_specs = [per_b((MLA_HEADS, QCAT)), per_b((1, QCAT)), per_b((1, SB_WIDTH)),
                pl.BlockSpec((PAGE_SIZE, PAGE_SIZE), lambda b, g, pt: (0, 0))]
    operands = [qcat_s, knew, sq_s, tri]
    for pool in (lat_pool, kr_pool, sbk_pool, sbv_pool):
        for r in range(group):
            in_specs.append(page_spec(pool, r))
            operands.append(pool)
    grid_spec = pltpu.PrefetchScalarGridSpec(
        num_scalar_prefetch=1, grid=(db, steps), in_specs=in_specs,
        out_specs=(per_b((MLA_HEADS, MLA_KV_LORA)), per_b((1, SB_WIDTH))),
        scratch_shapes=[pltpu.VMEM((MLA_HEADS, LANES), F32)] * 4 + [pltpu.VMEM((SB_HEADS, SB_WIDTH), F32)])
    return pl.pallas_call(
        functools.partial(_sample_attn_kernel, group=group),
        grid_spec=grid_spec,
        out_shape=(jax.ShapeDtypeStruct((db, MLA_HEADS, MLA_KV_LORA), F32),
                   jax.ShapeDtypeStruct((db, 1, SB_WIDTH), F32)),
        compiler_params=pltpu.CompilerParams(dimension_semantics=("arbitrary", "arbitrary"),
                                             vmem_limit_bytes=VMEM_LIMIT),
        name="sample_attn",
    )(page_table, *operands)


def _mix_kernel(x_ref, g_ref, wg_ref, olat_ref, wuv_ref, wmo_ref, osb_ref, wso_ref, wout_ref, g2_ref,
                h_ref, hn_ref):
    x = x_ref[...]
    xn = _rmsnorm(x, g_ref[...]).astype(BF16)
    gates = _dot(xn, wg_ref[...])
    olat = olat_ref[...].astype(BF16)
    om = jnp.concatenate([_dot(olat[:, p * QCAT:(p + 1) * QCAT], wuv_ref[p]) for p in range(MLA_HEADS // 2)],
                         axis=1).astype(BF16)
    o_mla = _dot(om, wmo_ref[...])
    o_sb = _dot(osb_ref[...].astype(BF16), wso_ref[...])
    mix = _sigmoid(gates[:, 0:D_MODEL]) * o_mla + _sigmoid(gates[:, D_MODEL:2 * D_MODEL]) * o_sb
    h = x + _dot(mix.astype(BF16), wout_ref[...])
    h_ref[...] = h
    hn_ref[...] = _rmsnorm(h, g2_ref[...]).astype(BF16)


def _mix(x, olat, osb, w, tm):
    m = x.shape[0]
    full = lambda a: pl.BlockSpec(a.shape, lambda i: (0,) * a.ndim)
    row = lambda n: pl.BlockSpec((tm, n), lambda i: (i, 0))
    ins = (x, w["attn_norm"], w["w_gate"], olat, w["w_uv_bd"], w["w_mla_o"], osb, w["w_sb_o"], w["w_out"],
           w["ffn_norm"])
    in_specs = [row(D_MODEL), full(ins[1]), full(ins[2]), row(olat.shape[1]), full(ins[4]), full(ins[5]),
                row(SB_WIDTH), full(ins[7]), full(ins[8]), full(ins[9])]
    return pl.pallas_call(
        _mix_kernel, grid=(m // tm,), in_specs=in_specs,
        out_specs=(row(D_MODEL), row(D_MODEL)),
        out_shape=(jax.ShapeDtypeStruct((m, D_MODEL), F32), jax.ShapeDtypeStruct((m, D_MODEL), BF16)),
        compiler_params=pltpu.CompilerParams(dimension_semantics=("arbitrary",), vmem_limit_bytes=VMEM_LIMIT),
        name="mix",
    )(*ins)


FFN_CHUNK = 256


SUBLANES = 8


def _ffn_chunks(hn, wup_ref, cw_ref, cb_ref, wdn_ref, shifted):
    acc = jnp.zeros((hn.shape[0], D_MODEL), F32)
    for c in range(D_FF // FFN_CHUNK):
        conv = []
        for half in range(2):
            lo = half * D_FF + c * FFN_CHUNK
            cols = slice(lo, lo + FFN_CHUNK)
            u = _dot(hn, wup_ref[:, cols])
            u1, u2 = shifted(u, cols)
            conv.append(cb_ref[:, cols] + cw_ref[0:1, cols] * u2 + cw_ref[1:2, cols] * u1 + cw_ref[2:3, cols] * u)
        a, b = conv
        gated = (a * _sigmoid(a) * b).astype(BF16)
        acc = acc + _dot(gated, wdn_ref[c * FFN_CHUNK:(c + 1) * FFN_CHUNK, :])
    return acc


def _ffn_out(h_ref, acc, gf_ref, y_ref, final):
    y = h_ref[...] + acc
    y_ref[...] = _rmsnorm(y, gf_ref[...]) if final else y


def _ffn_seq_kernel(hn_ref, h_ref, wup_ref, cw_ref, cb_ref, wdn_ref, gf_ref, y_ref, st_ref, tail_s, *,
                    tm, tiles_per_seq, final):
    @pl.when(pl.program_id(0) % tiles_per_seq == 0)
    def _():
        tail_s[...] = jnp.zeros(tail_s.shape, F32)

    rowid = lax.broadcasted_iota(jnp.int32, (tm, FFN_CHUNK), 0)

    def shifted(u, cols):
        p0, p1 = tail_s[SUBLANES - 2:SUBLANES - 1, cols], tail_s[SUBLANES - 1:SUBLANES, cols]
        u1 = jnp.where(rowid == 0, p1, pltpu.roll(u, 1, 0))
        u2 = jnp.where(rowid == 0, p0, jnp.where(rowid == 1, p1, pltpu.roll(u, 2, 0)))
        tail = u[tm - SUBLANES:tm]
        tail_s[:, cols] = tail
        st_ref[0, :, cols] = tail
        return u1, u2

    acc = _ffn_chunks(hn_ref[...], wup_ref, cw_ref, cb_ref, wdn_ref, shifted)
    _ffn_out(h_ref, acc, gf_ref, y_ref, final)


def _ffn_step_kernel(hn_ref, h_ref, wup_ref, cw_ref, cb_ref, wdn_ref, gf_ref, prev_ref, y_ref, st_ref, *, final):
    def shifted(u, cols):
        later = slice(2 * D_FF + cols.start, 2 * D_FF + cols.stop)
        p0, p1 = prev_ref[:, cols], prev_ref[:, later]
        st_ref[:, cols] = p1
        st_ref[:, later] = u
        return p1, p0

    acc = _ffn_chunks(hn_ref[...], wup_ref, cw_ref, cb_ref, wdn_ref, shifted)
    _ffn_out(h_ref, acc, gf_ref, y_ref, final)


def _ffn(hn, h, prev, w, tm, seq, final):
    m = hn.shape[0]
    full = lambda a: pl.BlockSpec(a.shape, lambda i: (0,) * a.ndim)
    row = lambda n: pl.BlockSpec((tm, n), lambda i: (i, 0))
    weights = (w["w_up"], w["conv_w"], w["conv_b"], w["w_down"], w["final_norm"])
    in_specs = [row(D_MODEL), row(D_MODEL)] + [full(a) for a in weights]
    params = pltpu.CompilerParams(dimension_semantics=("arbitrary",), vmem_limit_bytes=VMEM_LIMIT)
    y_shape = jax.ShapeDtypeStruct((m, D_MODEL), F32)
    if prev is None:
        assert seq % tm == 0 and tm >= SUBLANES
        tiles_per_seq = seq // tm
        y, st = pl.pallas_call(
            functools.partial(_ffn_seq_kernel, tm=tm, tiles_per_seq=tiles_per_seq, final=final),
            grid=(m // tm,), in_specs=in_specs,
            out_specs=(row(D_MODEL), pl.BlockSpec((1, SUBLANES, 2 * D_FF), lambda i: (i // tiles_per_seq, 0, 0))),
            out_shape=(y_shape, jax.ShapeDtypeStruct((m // seq, SUBLANES, 2 * D_FF), F32)),
            scratch_shapes=[pltpu.VMEM((SUBLANES, 2 * D_FF), F32)],
            compiler_params=params, name="ffn_seq",
        )(hn, h, *weights)
        return y, st[:, SUBLANES - (CONV_W - 1):]
    assert seq == 1
    st_w = (CONV_W - 1) * 2 * D_FF
    y, st = pl.pallas_call(
        functools.partial(_ffn_step_kernel, final=final),
        grid=(m // tm,), in_specs=in_specs + [row(st_w)],
        out_specs=(row(D_MODEL), row(st_w)),
        out_shape=(y_shape, jax.ShapeDtypeStruct((m, st_w), F32)),
        compiler_params=params, name="ffn_step",
    )(hn, h, *weights, prev.reshape(m, st_w))
    return y, st.reshape(m, CONV_W - 1, 2 * D_FF)


def _prep_weights(l, attn_norm, w_in, q_norm, kv_norm, w_uq, w_ukv, w_mla_o, w_sb_o, w_out, ffn_norm, w_up, conv_w,
                  conv_b, w_down, final_norm):
    w_in = w_in[l]
    o_kr = MLA_Q_LORA + MLA_KV_LORA
    o_sq = o_kr + MLA_ROPE_DIM
    o_g = o_sq + 3 * SB_WIDTH
    pad = jnp.zeros((D_MODEL, LANES - MLA_ROPE_DIM), w_in.dtype)
    w_in_p = jnp.concatenate([w_in[:, :o_sq], pad, w_in[:, o_sq:o_g]], axis=1).astype(BF16)
    w_gate = w_in[:, o_g:].astype(BF16)
    per_head = MLA_NOPE_DIM + MLA_ROPE_DIM
    uq3 = w_uq[l].reshape(MLA_Q_LORA, MLA_HEADS, per_head)
    uq_nope = uq3[:, :, :MLA_NOPE_DIM].reshape(MLA_Q_LORA, MLA_HEADS * MLA_NOPE_DIM)
    uq_rope = jnp.pad(uq3[:, :, MLA_NOPE_DIM:], ((0, 0), (0, 0), (0, LANES - MLA_ROPE_DIM)))
    w_uq2 = jnp.concatenate([uq_nope, uq_rope.reshape(MLA_Q_LORA, MLA_HEADS * LANES)], axis=1).astype(BF16)
    ukv3 = w_ukv[l].reshape(MLA_KV_LORA, MLA_HEADS, MLA_NOPE_DIM + MLA_V_DIM)
    uk_t = jnp.transpose(ukv3[:, :, :MLA_NOPE_DIM], (1, 2, 0))
    uv = jnp.transpose(ukv3[:, :, MLA_NOPE_DIM:], (1, 0, 2))
    zk = jnp.zeros_like(uk_t[0])
    zv = jnp.zeros_like(uv[0])
    w_uk_bd = jnp.stack([jnp.block([[uk_t[2 * p], zk], [zk, uk_t[2 * p + 1]]]) for p in range(MLA_HEADS // 2)])
    w_uv_bd = jnp.stack([jnp.block([[uv[2 * p], zv], [zv, uv[2 * p + 1]]]) for p in range(MLA_HEADS // 2)])
    return {
        "attn_norm": attn_norm[l][None, :], "w_in_p": w_in_p, "w_gate": w_gate,
        "q_norm": q_norm[l][None, :], "kv_norm": kv_norm[l][None, :], "w_uq2": w_uq2,
        "w_uk_bd": w_uk_bd.astype(BF16), "w_uv_bd": w_uv_bd.astype(BF16),
        "w_mla_o": w_mla_o[l].astype(BF16), "w_sb_o": w_sb_o[l].astype(BF16), "w_out": w_out[l].astype(BF16),
        "ffn_norm": ffn_norm[l][None, :], "w_up": w_up[l].astype(BF16), "conv_w": conv_w[l],
        "conv_b": conv_b[l][None, :], "w_down": w_down[l].astype(BF16), "final_norm": final_norm[None, :],
    }


def _rope_tables(pos):
    inv = ROPE_BASE ** (-jnp.arange(ROPE_HALF, dtype=F32) / ROPE_HALF)
    ang = pos.astype(F32)[:, None] * inv[None, :]
    cos, sin = jnp.cos(ang), jnp.sin(ang)
    z = jnp.zeros_like(cos)
    zpad = jnp.zeros((pos.shape[0], LANES - MLA_ROPE_DIM), F32)
    return (jnp.concatenate([cos, cos, zpad], axis=1), jnp.concatenate([z, sin, zpad], axis=1),
            jnp.concatenate([-sin, z, zpad], axis=1))


def _tri(n):
    r = lax.broadcasted_iota(jnp.int32, (n, n), 0)
    c = lax.broadcasted_iota(jnp.int32, (n, n), 1)
    return (r > c).astype(BF16)


def _pick_tile(n, target):
    t = min(n, target)
    assert n % t == 0
    return t


def kernel(x_prompt, x_sample, cache_latent, cache_krope, cache_sb_k, cache_sb_v, state_conv, page_table, attn_norm, w_in, q_norm, kv_norm, w_uq, w_ukv, w_mla_o, w_sb_o, w_out, ffn_norm, w_up, conv_w, conv_b, w_down, final_norm):
    batch, seq, _ = x_prompt.shape
    db, dec_seq, _ = x_sample.shape
    assert dec_seq == 1
    depth = w_in.shape[0]
    n_pool = cache_latent.shape[1]
    n_pages = page_table.shape[1]
    mp, ms = batch * seq, db * dec_seq
    tabs_p = _rope_tables(jnp.arange(seq))
    tabs_s = _rope_tables(jnp.full((ms,), n_pages * PAGE_SIZE, jnp.int32))
    sb_t = _pick_tile(seq, 256)
    tri_p, tri_s = _tri(sb_t), _tri(PAGE_SIZE)
    tm_p = _pick_tile(seq, 512)
    tm_f = _pick_tile(seq, 512)
    tm_s = _pick_tile(ms, 128)

    def keys_minor(pool):
        return jnp.transpose(pool, (0, 2, 3, 1)).reshape(n_pool, SB_WIDTH, PAGE_SIZE)

    hp = x_prompt.reshape(mp, D_MODEL)
    hs = x_sample.reshape(ms, D_MODEL)
    outs = [[] for _ in range(10)]
    for l in range(depth):
        final = l == depth - 1
        w = _prep_weights(l, attn_norm, w_in, q_norm, kv_norm, w_uq, w_ukv, w_mla_o, w_sb_o, w_out, ffn_norm, w_up,
                          conv_w, conv_b, w_down, final_norm)
        qcat, kcat, ckv, kr, sq, sk, sv, skb, svb = _project(hp, tabs_p, seq // tm_p, w, tm_p)
        olat = _mla_prompt(qcat, kcat, batch, seq, tq=_pick_tile(seq, 128), tk=_pick_tile(seq, 256))
        osb = _sb_prompt(sq, skb, svb, tri_p, batch, seq, t=sb_t)
        h, hn = _mix(hp, olat, osb, w, tm_p)
        hp, conv_st = _ffn(hn, h, None, w, tm_f, seq, final)
        outs[0].append(ckv.reshape(batch, seq, MLA_KV_LORA))
        outs[1].append(kr[:, :MLA_ROPE_DIM].reshape(batch, seq, MLA_ROPE_DIM))
        outs[2].append(sk.reshape(batch, seq, SB_HEADS, SB_HEAD_DIM))
        outs[3].append(sv.reshape(batch, seq, SB_HEADS, SB_HEAD_DIM))
        outs[4].append(conv_st)
        qcat, kcat, ckv, kr, sq, sk, sv, _, _ = _project(hs, tabs_s, 1, w, tm_s)
        olat, osb = _sample_attention(
            jnp.swapaxes(qcat, 0, 1), jnp.concatenate([ckv, kr], axis=1).reshape(ms, 1, QCAT),
            sq.reshape(ms, 1, SB_WIDTH), tri_s,
            cache_latent[l], jnp.swapaxes(cache_krope[l], 1, 2), keys_minor(cache_sb_k[l]), keys_minor(cache_sb_v[l]),
            page_table)
        h, hn = _mix(hs, olat.reshape(ms, MLA_HEADS * MLA_KV_LORA), osb.reshape(ms, SB_WIDTH), w, tm_s)
        hs, conv_st = _ffn(hn, h, state_conv[l], w, tm_s, 1, final)
        outs[5].append(ckv.reshape(db, dec_seq, MLA_KV_LORA))
        outs[6].append(kr[:, :MLA_ROPE_DIM].reshape(db, dec_seq, MLA_ROPE_DIM))
        outs[7].append(sk.reshape(db, dec_seq, SB_HEADS, SB_HEAD_DIM))
        outs[8].append(sv.reshape(db, dec_seq, SB_HEADS, SB_HEAD_DIM))
        outs[9].append(conv_st)
    return (hp.reshape(batch, seq, D_MODEL), hs.reshape(db, dec_seq, D_MODEL)) + tuple(jnp.stack(o) for o in outs)
```

```python
import functools

import jax
import jax.numpy as jnp
from jax import lax
from jax.experimental import pallas as pl
from jax.experimental.pallas import tpu as pltpu

D_MODEL = 1024
PAGE_SIZE = 128
MLA_HEADS = 8
MLA_NOPE_DIM = 64
MLA_ROPE_DIM = 32
MLA_V_DIM = 64
MLA_Q_LORA = 256
MLA_KV_LORA = 128
ROPE_BASE = 10000.0
MLA_SCALE = (MLA_NOPE_DIM + MLA_ROPE_DIM) ** -0.5
SB_HEADS = 8
SB_HEAD_DIM = 64
SB_WIDTH = SB_HEADS * SB_HEAD_DIM
SB_SCALE = SB_HEAD_DIM ** -0.5
D_FF = 2816
CONV_W = 3
NORM_EPS = 1e-6

LANES = 128
QCAT = 2 * LANES
ROPE_HALF = MLA_ROPE_DIM // 2
MASK_VALUE = -1e30
VMEM_LIMIT = 56 * 1024 * 1024

BF16 = jnp.bfloat16
F32 = jnp.float32

_NT = (((1,), (1,)), ((), ()))


def _dot(a, b):
    return jnp.dot(a, b, preferred_element_type=F32)


def _dot_nt(a, b):
    return lax.dot_general(a, b, _NT, preferred_element_type=F32)


def _rmsnorm(x, g):
    return x * lax.rsqrt(jnp.mean(x * x, axis=-1, keepdims=True) + NORM_EPS) * g


def _sigmoid(x):
    return 1.0 / (1.0 + jnp.exp(-x))


def _rope(x, c, s1, s2):
    return x * c + pltpu.roll(x, ROPE_HALF, 1) * s1 + pltpu.roll(x, LANES - ROPE_HALF, 1) * s2


LOG2E = 1.4426950408889634


def _log2_sigmoid_pair(z2):
    soft = jnp.log2(1.0 + jnp.exp2(-jnp.abs(z2)))
    ls = jnp.minimum(z2, 0.0) - soft
    return ls, ls - z2


def _split_bf16(x):
    hi = x.astype(BF16)
    lo = (x - hi.astype(F32)).astype(BF16)
    return hi, lo


def _proj_kernel(x_ref, g_ref, w_ref, qn_ref, kvn_ref, wuq_ref, wuk_ref, c_ref, s1_ref, s2_ref,
                 qcat_ref, kcat_ref, ckv_ref, kr_ref, sq_ref, sk_ref, sv_ref, skb_ref, svb_ref):
    xn = _rmsnorm(x_ref[...], g_ref[...]).astype(BF16)
    y = _dot(xn, w_ref[...])
    c, s1, s2 = c_ref[...], s1_ref[...], s2_ref[...]

    cqn = _rmsnorm(y[:, 0:MLA_Q_LORA], qn_ref[...]).astype(BF16)
    q2 = _dot(cqn, wuq_ref[...])
    nope_w = MLA_HEADS * MLA_NOPE_DIM
    for p in range(MLA_HEADS // 2):
        ql = _dot(q2[:, p * LANES:(p + 1) * LANES].astype(BF16), wuk_ref[p]) * (MLA_SCALE * LOG2E)
        for e in range(2):
            qcat_ref[2 * p + e, :, 0:LANES] = ql[:, e * LANES:(e + 1) * LANES].astype(BF16)
    for h in range(MLA_HEADS):
        xr = q2[:, nope_w + h * LANES:nope_w + (h + 1) * LANES]
        qcat_ref[h, :, LANES:QCAT] = (_rope(xr, c, s1, s2) * (MLA_SCALE * LOG2E)).astype(BF16)

    o = MLA_Q_LORA
    ckv = _rmsnorm(y[:, o:o + MLA_KV_LORA], kvn_ref[...])
    ckv_ref[...] = ckv
    kcat_ref[:, 0:LANES] = ckv.astype(BF16)
    o += MLA_KV_LORA
    kr = _rope(y[:, o:o + LANES], c, s1, s2)
    kr_ref[...] = kr
    kcat_ref[:, LANES:QCAT] = kr.astype(BF16)
    o += LANES
    sq_ref[...] = (y[:, o:o + SB_WIDTH] * (SB_SCALE * LOG2E)).astype(BF16)
    o += SB_WIDTH
    sk = y[:, o:o + SB_WIDTH]
    sk_ref[...] = sk
    skb_ref[...] = sk.astype(BF16)
    o += SB_WIDTH
    sv = y[:, o:o + SB_WIDTH]
    sv_ref[...] = sv
    svb_ref[...] = sv.astype(BF16)


def _project(x, tabs, n_tab_blocks, w, tm):
    m = x.shape[0]
    full = lambda a: pl.BlockSpec(a.shape, lambda i: (0,) * a.ndim)
    row = lambda n: pl.BlockSpec((tm, n), lambda i: (i, 0))
    tab = pl.BlockSpec((tm, LANES), lambda i: (i % n_tab_blocks, 0))
    out_shape = (
        jax.ShapeDtypeStruct((MLA_HEADS, m, QCAT), BF16),
        jax.ShapeDtypeStruct((m, QCAT), BF16),
        jax.ShapeDtypeStruct((m, MLA_KV_LORA), F32),
        jax.ShapeDtypeStruct((m, LANES), F32),
        jax.ShapeDtypeStruct((m, SB_WIDTH), BF16),
        jax.ShapeDtypeStruct((m, SB_WIDTH), F32),
        jax.ShapeDtypeStruct((m, SB_WIDTH), F32),
        jax.ShapeDtypeStruct((m, SB_WIDTH), BF16),
        jax.ShapeDtypeStruct((m, SB_WIDTH), BF16),
    )
    out_specs = (
        pl.BlockSpec((MLA_HEADS, tm, QCAT), lambda i: (0, i, 0)),
        row(QCAT), row(MLA_KV_LORA), row(LANES), row(SB_WIDTH), row(SB_WIDTH), row(SB_WIDTH),
        row(SB_WIDTH), row(SB_WIDTH),
    )
    ins = (x, w["attn_norm"], w["w_in_p"], w["q_norm"], w["kv_norm"], w["w_uq2"], w["w_uk_bd"]) + tabs
    in_specs = [row(D_MODEL)] + [full(a) for a in ins[1:7]] + [tab, tab, tab]
    return pl.pallas_call(
        _proj_kernel, grid=(m // tm,), in_specs=in_specs, out_specs=out_specs, out_shape=out_shape,
        compiler_params=pltpu.CompilerParams(dimension_semantics=("arbitrary",), vmem_limit_bytes=VMEM_LIMIT),
        name="proj",
    )(*ins)


MLA_ROW_GROUPS = 2


def _mla_kernel(q_ref, k_ref, o_ref, m_s, l_s, acc_s, *, tq, tk):
    i = pl.program_id(1)
    rows = MLA_HEADS * tq
    q = q_ref[...].reshape(rows, QCAT)
    m_s[...] = jnp.full((rows, LANES), MASK_VALUE, F32)
    l_s[...] = jnp.zeros((rows, LANES), F32)
    acc_s[...] = jnp.zeros((rows, LANES), F32)

    gr = rows // MLA_ROW_GROUPS
    groups = [slice(n * gr, (n + 1) * gr) for n in range(MLA_ROW_GROUPS)]

    def block(j, masked):
        k = k_ref[pl.ds(pl.multiple_of(j * tk, tk), tk), :]
        s = [_dot_nt(q[g], k) for g in groups]
        if masked:
            qpos = i * tq + jnp.bitwise_and(lax.broadcasted_iota(jnp.int32, (gr, tk), 0), tq - 1)
            kpos = j * tk + lax.broadcasted_iota(jnp.int32, (gr, tk), 1)
            s = [jnp.where(kpos <= qpos, x, MASK_VALUE) for x in s]
        m_prev = [m_s[g] for g in groups]
        m_next = [jnp.maximum(mp, jnp.max(x, axis=1, keepdims=True)) for mp, x in zip(m_prev, s)]
        alpha = [jnp.exp2(mp - mn) for mp, mn in zip(m_prev, m_next)]
        p = [jnp.concatenate([jnp.exp2(x[:, c * LANES:(c + 1) * LANES] - mn) for c in range(tk // LANES)], axis=1)
             for x, mn in zip(s, m_next)]
        pv = [_dot(x.astype(BF16), k[:, 0:MLA_KV_LORA]) for x in p]
        for n, g in enumerate(groups):
            l_s[g] = alpha[n] * l_s[g] + jnp.sum(p[n], axis=1, keepdims=True)
            acc_s[g] = alpha[n] * acc_s[g] + pv[n]
            m_s[g] = m_next[n]

    n_full = (i * tq) // tk

    def body(j, carry):
        block(j, False)
        return carry

    lax.fori_loop(0, n_full, body, 0)
    block(n_full, True)
    o = acc_s[...] / l_s[...]
    for h in range(MLA_HEADS):
        o_ref[:, h * LANES:(h + 1) * LANES] = o[h * tq:(h + 1) * tq].astype(o_ref.dtype)


def _mla_prompt(qcat, kcat, batch, seq, tq=128, tk=256):
    assert tk % tq == 0 and seq % tk == 0 and tq & (tq - 1) == 0
    m = batch * seq
    nq = seq // tq
    rows = MLA_HEADS * tq
    return pl.pallas_call(
        functools.partial(_mla_kernel, tq=tq, tk=tk),
        grid=(batch, nq),
        in_specs=[pl.BlockSpec((MLA_HEADS, tq, QCAT), lambda b, i: (0, b * nq + i, 0)),
                  pl.BlockSpec((seq, QCAT), lambda b, i: (b, 0))],
        out_specs=pl.BlockSpec((tq, MLA_HEADS * MLA_KV_LORA), lambda b, i: (b * nq + i, 0)),
        out_shape=jax.ShapeDtypeStruct((m, MLA_HEADS * MLA_KV_LORA), BF16),
        scratch_shapes=[pltpu.VMEM((rows, LANES), F32)] * 3,
        compiler_params=pltpu.CompilerParams(dimension_semantics=("arbitrary", "arbitrary"),
                                             vmem_limit_bytes=VMEM_LIMIT),
        name="mla_prompt",
    )(qcat, kcat)


def _sb_kernel(q_ref, k_ref, v_ref, t_ref, o_ref, q2_s, acc_s, carry_s, *, t, pairs):
    i = pl.program_id(2)
    tri = t_ref[...]
    lane = lax.broadcasted_iota(jnp.int32, (t, LANES), 1)
    first = lane < SB_HEAD_DIM
    row = jnp.bitwise_and(lax.broadcasted_iota(jnp.int32, (2 * t, t), 0), t - 1)
    strict = lax.broadcasted_iota(jnp.int32, (2 * t, t), 1) < row
    for p in range(pairs):
        q = q_ref[:, p * LANES:(p + 1) * LANES].astype(F32)
        q2_s[p, 0:t, :] = jnp.where(first, q, 0.0).astype(BF16)
        q2_s[p, t:2 * t, :] = jnp.where(first, 0.0, q).astype(BF16)
    acc_s[...] = jnp.zeros(acc_s.shape, F32)
    carry_s[...] = jnp.zeros(carry_s.shape, F32)

    def block(j, masked):
        start = pl.multiple_of(j * t, t)
        ps = range(pairs)
        z = [_dot_nt(q2_s[p], k_ref[pl.ds(start, t), p * LANES:(p + 1) * LANES]) for p in ps]
        ls, lk = zip(*[_log2_sigmoid_pair(z[p]) for p in ps])
        if masked:
            lk = [jnp.where(strict, lk[p], 0.0) for p in ps]
        split = [_split_bf16(lk[p]) for p in ps]
        after = [_dot(split[p][0], tri) + _dot(split[p][1], tri) for p in ps]
        carry = [carry_s[p] for p in ps]
        log_a = [jnp.concatenate(
            [ls[p][:, c * LANES:(c + 1) * LANES] + after[p][:, c * LANES:(c + 1) * LANES] + carry[p]
             for c in range(t // LANES)], axis=1) for p in ps]
        if masked:
            log_a = [jnp.where(strict, log_a[p], MASK_VALUE) for p in ps]
        a = [jnp.exp2(log_a[p]).astype(BF16) for p in ps]
        pv = [_dot(a[p], v_ref[pl.ds(start, t), p * LANES:(p + 1) * LANES]) for p in ps]
        for p in ps:
            acc_s[p] += pv[p]
            carry_s[p] = carry[p] + jnp.sum(lk[p], axis=1, keepdims=True)

    block(i, True)

    def body(n, c):
        block(i - 1 - n, False)
        return c

    lax.fori_loop(0, i, body, 0)
    for p in range(pairs):
        o_ref[:, p * LANES:(p + 1) * LANES] = jnp.where(first, acc_s[p, 0:t, :], acc_s[p, t:2 * t, :]).astype(o_ref.dtype)


def _sb_prompt(sq, skb, svb, tri, batch, seq, t=256, pairs=4):
    assert seq % t == 0 and t & (t - 1) == 0 and (SB_HEADS // 2) % pairs == 0
    m = batch * seq
    nq = seq // t
    w = pairs * LANES
    return pl.pallas_call(
        functools.partial(_sb_kernel, t=t, pairs=pairs),
        grid=(batch, SB_HEADS // 2 // pairs, nq),
        in_specs=[pl.BlockSpec((t, w), lambda b, p, i: (b * nq + i, p)),
                  pl.BlockSpec((seq, w), lambda b, p, i: (b, p)),
                  pl.BlockSpec((seq, w), lambda b, p, i: (b, p)),
                  pl.BlockSpec((t, t), lambda b, p, i: (0, 0))],
        out_specs=pl.BlockSpec((t, w), lambda b, p, i: (b * nq + i, p)),
        out_shape=jax.ShapeDtypeStruct((m, SB_WIDTH), BF16),
        scratch_shapes=[pltpu.VMEM((pairs, 2 * t, LANES), BF16), pltpu.VMEM((pairs, 2 * t, LANES), F32),
                        pltpu.VMEM((pairs, 2 * t, LANES), F32)],
        compiler_params=pltpu.CompilerParams(dimension_semantics=("arbitrary",) * 3,
                                             vmem_limit_bytes=VMEM_LIMIT),
        name="sb_prompt",
    )(sq, skb, svb, tri)


def _sample_attn_kernel(pt_ref, qcat_ref, knew_ref, sq_ref, t_ref, *rest, group):
    del pt_ref
    lat_refs = rest[0:group]
    kr_refs = rest[group:2 * group]
    sbk_refs = rest[2 * group:3 * group]
    sbv_refs = rest[3 * group:4 * group]
    olat_ref, osb_ref, m_s, l_s, acc_s, carry_s, sbacc_s = rest[4 * group:]
    g = pl.program_id(1)
    qc = qcat_ref[...]

    @pl.when(g == 0)
    def _():
        kn = knew_ref[...].astype(F32)
        s_self = jnp.sum(qc.astype(F32) * kn, axis=1, keepdims=True)
        m_s[...] = jnp.broadcast_to(s_self, (MLA_HEADS, LANES))
        l_s[...] = jnp.ones((MLA_HEADS, LANES), F32)
        acc_s[...] = jnp.broadcast_to(kn[:, 0:MLA_KV_LORA], (MLA_HEADS, MLA_KV_LORA))
        carry_s[...] = jnp.zeros((SB_HEADS, LANES), F32)
        sbacc_s[...] = jnp.zeros((SB_HEADS, SB_WIDTH), F32)

    q_lat = qc[:, 0:LANES]
    q_rope = qc[:, LANES:LANES + MLA_ROPE_DIM]
    head_of_lane = lax.broadcasted_iota(jnp.int32, (SB_HEADS, SB_WIDTH), 1) // SB_HEAD_DIM
    own_head = head_of_lane == lax.broadcasted_iota(jnp.int32, (SB_HEADS, SB_WIDTH), 0)
    sq = jnp.broadcast_to(sq_ref[...].astype(F32), (SB_HEADS, SB_WIDTH))
    q_sb = jnp.where(own_head, sq, 0.0).astype(BF16)
    tri = t_ref[...]

    pages = range(group)
    lat = [lat_refs[r][...].astype(BF16) for r in pages]
    s = [_dot_nt(q_lat, lat[r]) + _dot(q_rope, kr_refs[r][...].astype(BF16)) for r in pages]
    m = m_s[...]
    m_next = jnp.maximum(m, jnp.max(functools.reduce(jnp.maximum, s), axis=1, keepdims=True))
    alpha = jnp.exp2(m - m_next)
    p = [jnp.exp2(s[r] - m_next) for r in pages]
    l = alpha * l_s[...] + jnp.sum(functools.reduce(jnp.add, p), axis=1, keepdims=True)
    acc = alpha * acc_s[...] + functools.reduce(jnp.add, [_dot(p[r].astype(BF16), lat[r]) for r in pages])
    m_s[...], l_s[...], acc_s[...] = m_next, l, acc

    z = jnp.concatenate([_dot(q_sb, sbk_refs[r][...].astype(BF16)) for r in pages], axis=0)
    ls, lk = _log2_sigmoid_pair(z)
    hi, lo = _split_bf16(lk)
    after = _dot(hi, tri) + _dot(lo, tri)
    page_sum = jnp.sum(lk, axis=1, keepdims=True)
    log_a = ls + after
    carry = carry_s[...]
    pv = []
    for r in pages:
        rows = slice(r * SB_HEADS, (r + 1) * SB_HEADS)
        a = jnp.exp2(log_a[rows] + carry)
        pv.append(_dot_nt(a.astype(BF16), sbv_refs[r][...].astype(BF16)))
        carry = carry + page_sum[rows]
    sbacc = sbacc_s[...] + functools.reduce(jnp.add, pv)
    carry_s[...], sbacc_s[...] = carry, sbacc

    @pl.when(g == pl.num_programs(1) - 1)
    def _():
        olat_ref[...] = acc / l
        osb_ref[...] = jnp.sum(jnp.where(own_head, sbacc, 0.0), axis=0, keepdims=True)


def _sample_attention(qcat_s, knew, sq_s, tri, lat_pool, kr_pool, sbk_pool, sbv_pool, page_table, group=16):
    db, n_pages = page_table.shape
    assert n_pages % group == 0
    steps = n_pages // group

    def page_spec(pool, r):
        return pl.BlockSpec((None,) + pool.shape[1:],
                            lambda b, g, pt: (pt[b, n_pages - 1 - (g * group + r)], 0, 0))

    per_b = lambda shape: pl.BlockSpec((None,) + shape, lambda b, g, pt: (b, 0, 0))
    in_specs = [per_b((MLA_HEADS, QCAT)), per_b((1, QCAT)), per_b((1, SB_WIDTH)),
                pl.BlockSpec((PAGE_SIZE, PAGE_SIZE), lambda b, g, pt: (0, 0))]
    operands = [qcat_s, knew, sq_s, tri]
    for pool in (lat_pool, kr_pool, sbk_pool, sbv_pool):
        for r in range(group):
            in_specs.append(page_spec(pool, r))
            operands.append(pool)
    grid_spec = pltpu.PrefetchScalarGridSpec(
        num_scalar_prefetch=1, grid=(db, steps), in_specs=in_specs,
        out_specs=(per_b((MLA_HEADS, MLA_KV_LORA)), per_b((1, SB_WIDTH))),
        scratch_shapes=[pltpu.VMEM((MLA_HEADS, LANES), F32)] * 4 + [pltpu.VMEM((SB_HEADS, SB_WIDTH), F32)])
    return pl.pallas_call(
        functools.partial(_sample_attn_kernel, group=group),
        grid_spec=grid_spec,
        out_shape=(jax.ShapeDtypeStruct((db, MLA_HEADS, MLA_KV_LORA), F32),
                   jax.ShapeDtypeStruct((db, 1, SB_WIDTH), F32)),
        compiler_params=pltpu.CompilerParams(dimension_semantics=("arbitrary", "arbitrary"),
                                             vmem_limit_bytes=VMEM_LIMIT),
        name="sample_attn",
    )(page_table, *operands)


def _mix_kernel(x_ref, g_ref, wg_ref, olat_ref, wuv_ref, wmo_ref, osb_ref, wso_ref, wout_ref, g2_ref,
                h_ref, hn_ref):
    x = x_ref[...]
    xn = _rmsnorm(x, g_ref[...]).astype(BF16)
    gates = _dot(xn, wg_ref[...])
    olat = olat_ref[...].astype(BF16)
    om = jnp.concatenate([_dot(olat[:, p * QCAT:(p + 1) * QCAT], wuv_ref[p]) for p in range(MLA_HEADS // 2)],
                         axis=1).astype(BF16)
    o_mla = _dot(om, wmo_ref[...])
    o_sb = _dot(osb_ref[...].astype(BF16), wso_ref[...])
    mix = _sigmoid(gates[:, 0:D_MODEL]) * o_mla + _sigmoid(gates[:, D_MODEL:2 * D_MODEL]) * o_sb
    h = x + _dot(mix.astype(BF16), wout_ref[...])
    h_ref[...] = h
    hn_ref[...] = _rmsnorm(h, g2_ref[...]).astype(BF16)


def _mix(x, olat, osb, w, tm):
    m = x.shape[0]
    full = lambda a: pl.BlockSpec(a.shape, lambda i: (0,) * a.ndim)
    row = lambda n: pl.BlockSpec((tm, n), lambda i: (i, 0))
    ins = (x, w["attn_norm"], w["w_gate"], olat, w["w_uv_bd"], w["w_mla_o"], osb, w["w_sb_o"], w["w_out"],
           w["ffn_norm"])
    in_specs = [row(D_MODEL), full(ins[1]), full(ins[2]), row(olat.shape[1]), full(ins[4]), full(ins[5]),
                row(SB_WIDTH), full(ins[7]), full(ins[8]), full(ins[9])]
    return pl.pallas_call(
        _mix_kernel, grid=(m // tm,), in_specs=in_specs,
        out_specs=(row(D_MODEL), row(D_MODEL)),
        out_shape=(jax.ShapeDtypeStruct((m, D_MODEL), F32), jax.ShapeDtypeStruct((m, D_MODEL), BF16)),
        compiler_params=pltpu.CompilerParams(dimension_semantics=("arbitrary",), vmem_limit_bytes=VMEM_LIMIT),
        name="mix",
    )(*ins)


FFN_CHUNK = 256


SUBLANES = 8


def _ffn_chunks(hn, wup_ref, cw_ref, cb_ref, wdn_ref, shifted):
    n_chunks = D_FF // FFN_CHUNK

    def up(c):
        cols = [slice(half * D_FF + c * FFN_CHUNK, half * D_FF + (c + 1) * FFN_CHUNK) for half in range(2)]
        return [(_dot(hn, wup_ref[:, s]), s) for s in cols]

    acc = jnp.zeros((hn.shape[0], D_MODEL), F32)
    ahead = 2
    queue = [up(c) for c in range(ahead)]
    for c in range(n_chunks):
        cur = queue.pop(0)
        if c + ahead < n_chunks:
            queue.append(up(c + ahead))
        conv = []
        for u, cols in cur:
            u1, u2 = shifted(u, cols)
            conv.append(cb_ref[:, cols] + cw_ref[0:1, cols] * u2 + cw_ref[1:2, cols] * u1 + cw_ref[2:3, cols] * u)
        a, b = conv
        gated = (a * _sigmoid(a) * b).astype(BF16)
        acc = acc + _dot(gated, wdn_ref[c * FFN_CHUNK:(c + 1) * FFN_CHUNK, :])
    return acc


def _ffn_out(h_ref, acc, gf_ref, y_ref, final):
    y = h_ref[...] + acc
    y_ref[...] = _rmsnorm(y, gf_ref[...]) if final else y


def _ffn_seq_kernel(hn_ref, h_ref, wup_ref, cw_ref, cb_ref, wdn_ref, gf_ref, y_ref, st_ref, tail_s, *,
                    tm, tiles_per_seq, final):
    @pl.when(pl.program_id(0) % tiles_per_seq == 0)
    def _():
        tail_s[...] = jnp.zeros(tail_s.shape, F32)

    rowid = lax.broadcasted_iota(jnp.int32, (tm, FFN_CHUNK), 0)

    def shifted(u, cols):
        p0, p1 = tail_s[SUBLANES - 2:SUBLANES - 1, cols], tail_s[SUBLANES - 1:SUBLANES, cols]
        u1 = jnp.where(rowid == 0, p1, pltpu.roll(u, 1, 0))
        u2 = jnp.where(rowid == 0, p0, jnp.where(rowid == 1, p1, pltpu.roll(u, 2, 0)))
        tail = u[tm - SUBLANES:tm]
        tail_s[:, cols] = tail
        st_ref[0, :, cols] = tail
        return u1, u2

    acc = _ffn_chunks(hn_ref[...], wup_ref, cw_ref, cb_ref, wdn_ref, shifted)
    _ffn_out(h_ref, acc, gf_ref, y_ref, final)


def _ffn_step_kernel(hn_ref, h_ref, wup_ref, cw_ref, cb_ref, wdn_ref, gf_ref, prev_ref, y_ref, st_ref, *, final):
    def shifted(u, cols):
        later = slice(2 * D_FF + cols.start, 2 * D_FF + cols.stop)
        p0, p1 = prev_ref[:, cols], prev_ref[:, later]
        st_ref[:, cols] = p1
        st_ref[:, later] = u
        return p1, p0

    acc = _ffn_chunks(hn_ref[...], wup_ref, cw_ref, cb_ref, wdn_ref, shifted)
    _ffn_out(h_ref, acc, gf_ref, y_ref, final)


def _ffn(hn, h, prev, w, tm, seq, final):
    m = hn.shape[0]
    full = lambda a: pl.BlockSpec(a.shape, lambda i: (0,) * a.ndim)
    row = lambda n: pl.BlockSpec((tm, n), lambda i: (i, 0))
    weights = (w["w_up"], w["conv_w"], w["conv_b"], w["w_down"], w["final_norm"])
    in_specs = [row(D_MODEL), row(D_MODEL)] + [full(a) for a in weights]
    params = pltpu.CompilerParams(dimension_semantics=("arbitrary",), vmem_limit_bytes=VMEM_LIMIT)
    y_shape = jax.ShapeDtypeStruct((m, D_MODEL), F32)
    if prev is None:
        assert seq % tm == 0 and tm >= SUBLANES
        tiles_per_seq = seq // tm
        y, st = pl.pallas_call(
            functools.partial(_ffn_seq_kernel, tm=tm, tiles_per_seq=tiles_per_seq, final=final),
            grid=(m // tm,), in_specs=in_specs,
            out_specs=(row(D_MODEL), pl.BlockSpec((1, SUBLANES, 2 * D_FF), lambda i: (i // tiles_per_seq, 0, 0))),
            out_shape=(y_shape, jax.ShapeDtypeStruct((m // seq, SUBLANES, 2 * D_FF), F32)),
            scratch_shapes=[pltpu.VMEM((SUBLANES, 2 * D_FF), F32)],
            compiler_params=params, name="ffn_seq",
        )(hn, h, *weights)
        return y, st[:, SUBLANES - (CONV_W - 1):]
    assert seq == 1
    st_w = (CONV_W - 1) * 2 * D_FF
    y, st = pl.pallas_call(
        functools.partial(_ffn_step_kernel, final=final),
        grid=(m // tm,), in_specs=in_specs + [row(st_w)],
        out_specs=(row(D_MODEL), row(st_w)),
        out_shape=(y_shape, jax.ShapeDtypeStruct((m, st_w), F32)),
        compiler_params=params, name="ffn_step",
    )(hn, h, *weights, prev.reshape(m, st_w))
    return y, st.reshape(m, CONV_W - 1, 2 * D_FF)


def _prep_weights(l, attn_norm, w_in, q_norm, kv_norm, w_uq, w_ukv, w_mla_o, w_sb_o, w_out, ffn_norm, w_up, conv_w,
                  conv_b, w_down, final_norm):
    w_in = w_in[l]
    o_kr = MLA_Q_LORA + MLA_KV_LORA
    o_sq = o_kr + MLA_ROPE_DIM
    o_g = o_sq + 3 * SB_WIDTH
    pad = jnp.zeros((D_MODEL, LANES - MLA_ROPE_DIM), w_in.dtype)
    w_in_p = jnp.concatenate([w_in[:, :o_sq], pad, w_in[:, o_sq:o_g]], axis=1).astype(BF16)
    w_gate = w_in[:, o_g:].astype(BF16)
    per_head = MLA_NOPE_DIM + MLA_ROPE_DIM
    uq3 = w_uq[l].reshape(MLA_Q_LORA, MLA_HEADS, per_head)
    uq_nope = uq3[:, :, :MLA_NOPE_DIM].reshape(MLA_Q_LORA, MLA_HEADS * MLA_NOPE_DIM)
    uq_rope = jnp.pad(uq3[:, :, MLA_NOPE_DIM:], ((0, 0), (0, 0), (0, LANES - MLA_ROPE_DIM)))
    w_uq2 = jnp.concatenate([uq_nope, uq_rope.reshape(MLA_Q_LORA, MLA_HEADS * LANES)], axis=1).astype(BF16)
    ukv3 = w_ukv[l].reshape(MLA_KV_LORA, MLA_HEADS, MLA_NOPE_DIM + MLA_V_DIM)
    uk_t = jnp.transpose(ukv3[:, :, :MLA_NOPE_DIM], (1, 2, 0))
    uv = jnp.transpose(ukv3[:, :, MLA_NOPE_DIM:], (1, 0, 2))
    zk = jnp.zeros_like(uk_t[0])
    zv = jnp.zeros_like(uv[0])
    w_uk_bd = jnp.stack([jnp.block([[uk_t[2 * p], zk], [zk, uk_t[2 * p + 1]]]) for p in range(MLA_HEADS // 2)])
    w_uv_bd = jnp.stack([jnp.block([[uv[2 * p], zv], [zv, uv[2 * p + 1]]]) for p in range(MLA_HEADS // 2)])
    return {
        "attn_norm": attn_norm[l][None, :], "w_in_p": w_in_p, "w_gate": w_gate,
        "q_norm": q_norm[l][None, :], "kv_norm": kv_norm[l][None, :], "w_uq2": w_uq2,
        "w_uk_bd": w_uk_bd.astype(BF16), "w_uv_bd": w_uv_bd.astype(BF16),
        "w_mla_o": w_mla_o[l].astype(BF16), "w_sb_o": w_sb_o[l].astype(BF16), "w_out": w_out[l].astype(BF16),
        "ffn_norm": ffn_norm[l][None, :], "w_up": w_up[l].astype(BF16), "conv_w": conv_w[l],
        "conv_b": conv_b[l][None, :], "w_down": w_down[l].astype(BF16), "final_norm": final_norm[None, :],
    }


def _rope_tables(pos):
    inv = ROPE_BASE ** (-jnp.arange(ROPE_HALF, dtype=F32) / ROPE_HALF)
    ang = pos.astype(F32)[:, None] * inv[None, :]
    cos, sin = jnp.cos(ang), jnp.sin(ang)
    z = jnp.zeros_like(cos)
    zpad = jnp.zeros((pos.shape[0], LANES - MLA_ROPE_DIM), F32)
    return (jnp.concatenate([cos, cos, zpad], axis=1), jnp.concatenate([z, sin, zpad], axis=1),
            jnp.concatenate([-sin, z, zpad], axis=1))


def _tri(n):
    r = lax.broadcasted_iota(jnp.int32, (n, n), 0)
    c = lax.broadcasted_iota(jnp.int32, (n, n), 1)
    return (r > c).astype(BF16)


def _pick_tile(n, target):
    t = min(n, target)
    assert n % t == 0
    return t


def kernel(x_prompt, x_sample, cache_latent, cache_krope, cache_sb_k, cache_sb_v, state_conv, page_table, attn_norm, w_in, q_norm, kv_norm, w_uq, w_ukv, w_mla_o, w_sb_o, w_out, ffn_norm, w_up, conv_w, conv_b, w_down, final_norm):
    batch, seq, _ = x_prompt.shape
    db, dec_seq, _ = x_sample.shape
    assert dec_seq == 1
    depth = w_in.shape[0]
    n_pool = cache_latent.shape[1]
    n_pages = page_table.shape[1]
    mp, ms = batch * seq, db * dec_seq
    tabs_p = _rope_tables(jnp.arange(seq))
    tabs_s = _rope_tables(jnp.full((ms,), n_pages * PAGE_SIZE, jnp.int32))
    sb_t = _pick_tile(seq, 256)
    tri_p, tri_s = _tri(sb_t), _tri(PAGE_SIZE)
    tm_p = _pick_tile(seq, 512)
    tm_f = _pick_tile(seq, 512)
    tm_s = _pick_tile(ms, 128)

    def keys_minor(pool):
        return jnp.transpose(pool, (0, 2, 3, 1)).reshape(n_pool, SB_WIDTH, PAGE_SIZE)

    hp = x_prompt.reshape(mp, D_MODEL)
    hs = x_sample.reshape(ms, D_MODEL)
    outs = [[] for _ in range(10)]
    for l in range(depth):
        final = l == depth - 1
        w = _prep_weights(l, attn_norm, w_in, q_norm, kv_norm, w_uq, w_ukv, w_mla_o, w_sb_o, w_out, ffn_norm, w_up,
                          conv_w, conv_b, w_down, final_norm)
        qcat, kcat, ckv, kr, sq, sk, sv, skb, svb = _project(hp, tabs_p, seq // tm_p, w, tm_p)
        olat = _mla_prompt(qcat, kcat, batch, seq, tq=_pick_tile(seq, 256), tk=_pick_tile(seq, 256))
        osb = _sb_prompt(sq, skb, svb, tri_p, batch, seq, t=sb_t)
        h, hn = _mix(hp, olat, osb, w, tm_p)
        hp, conv_st = _ffn(hn, h, None, w, tm_f, seq, final)
        outs[0].append(ckv.reshape(batch, seq, MLA_KV_LORA))
        outs[1].append(kr[:, :MLA_ROPE_DIM].reshape(batch, seq, MLA_ROPE_DIM))
        outs[2].append(sk.reshape(batch, seq, SB_HEADS, SB_HEAD_DIM))
        outs[3].append(sv.reshape(batch, seq, SB_HEADS, SB_HEAD_DIM))
        outs[4].append(conv_st)
        qcat, kcat, ckv, kr, sq, sk, sv, _, _ = _project(hs, tabs_s, 1, w, tm_s)
        olat, osb = _sample_attention(
            jnp.swapaxes(qcat, 0, 1), jnp.concatenate([ckv, kr], axis=1).reshape(ms, 1, QCAT),
            sq.reshape(ms, 1, SB_WIDTH), tri_s,
            cache_latent[l], jnp.swapaxes(cache_krope[l], 1, 2), keys_minor(cache_sb_k[l]), keys_minor(cache_sb_v[l]),
            page_table)
        h, hn = _mix(hs, olat.reshape(ms, MLA_HEADS * MLA_KV_LORA), osb.reshape(ms, SB_WIDTH), w, tm_s)
        hs, conv_st = _ffn(hn, h, state_conv[l], w, tm_s, 1, final)
        outs[5].append(ckv.reshape(db, dec_seq, MLA_KV_LORA))
        outs[6].append(kr[:, :MLA_ROPE_DIM].reshape(db, dec_seq, MLA_ROPE_DIM))
        outs[7].append(sk.reshape(db, dec_seq, SB_HEADS, SB_HEAD_DIM))
        outs[8].append(sv.reshape(db, dec_seq, SB_HEADS, SB_HEAD_DIM))
        outs[9].append(conv_st)
    return (hp.reshape(batch, seq, D_MODEL), hs.reshape(db, dec_seq, D_MODEL)) + tuple(jnp.stack(o) for o in outs)
```

```python
import functools

import jax
import jax.numpy as jnp
from jax import lax
from jax.experimental import pallas as pl
from jax.experimental.pallas import tpu as pltpu

D_MODEL = 1024
PAGE_SIZE = 128
MLA_HEADS = 8
MLA_NOPE_DIM = 64
MLA_ROPE_DIM = 32
MLA_V_DIM = 64
MLA_Q_LORA = 256
MLA_KV_LORA = 128
ROPE_BASE = 10000.0
MLA_SCALE = (MLA_NOPE_DIM + MLA_ROPE_DIM) ** -0.5
SB_HEADS = 8
SB_HEAD_DIM = 64
SB_WIDTH = SB_HEADS * SB_HEAD_DIM
SB_SCALE = SB_HEAD_DIM ** -0.5
D_FF = 2816
CONV_W = 3
NORM_EPS = 1e-6

LANES = 128
QCAT = 2 * LANES
ROPE_HALF = MLA_ROPE_DIM // 2
MASK_VALUE = -1e30
VMEM_LIMIT = 56 * 1024 * 1024

BF16 = jnp.bfloat16
F32 = jnp.float32

_NT = (((1,), (1,)), ((), ()))


def _dot(a, b):
    return jnp.dot(a, b, preferred_element_type=F32)


def _dot_nt(a, b):
    return lax.dot_general(a, b, _NT, preferred_element_type=F32)


def _rmsnorm(x, g):
    return x * lax.rsqrt(jnp.mean(x * x, axis=-1, keepdims=True) + NORM_EPS) * g


def _sigmoid(x):
    return 1.0 / (1.0 + jnp.exp(-x))


def _rope(x, c, s1, s2):
    return x * c + pltpu.roll(x, ROPE_HALF, 1) * s1 + pltpu.roll(x, LANES - ROPE_HALF, 1) * s2


LOG2E = 1.4426950408889634


def _log2_sigmoid_pair(z2):
    soft = jnp.log2(1.0 + jnp.exp2(-jnp.abs(z2)))
    ls = jnp.minimum(z2, 0.0) - soft
    return ls, ls - z2


def _split_bf16(x):
    hi = x.astype(BF16)
    lo = (x - hi.astype(F32)).astype(BF16)
    return hi, lo


def _proj_kernel(x_ref, g_ref, w_ref, qn_ref, kvn_ref, wuq_ref, wuk_ref, c_ref, s1_ref, s2_ref,
                 qcat_ref, kcat_ref, ckv_ref, kr_ref, sq_ref, sk_ref, sv_ref, skb_ref, svb_ref):
    xn = _rmsnorm(x_ref[...], g_ref[...]).astype(BF16)
    y = _dot(xn, w_ref[...])
    c, s1, s2 = c_ref[...], s1_ref[...], s2_ref[...]

    cqn = _rmsnorm(y[:, 0:MLA_Q_LORA], qn_ref[...]).astype(BF16)
    q2 = _dot(cqn, wuq_ref[...])
    nope_w = MLA_HEADS * MLA_NOPE_DIM
    for p in range(MLA_HEADS // 2):
        ql = _dot(q2[:, p * LANES:(p + 1) * LANES].astype(BF16), wuk_ref[p]) * (MLA_SCALE * LOG2E)
        for e in range(2):
            qcat_ref[2 * p + e, :, 0:LANES] = ql[:, e * LANES:(e + 1) * LANES].astype(BF16)
    for h in range(MLA_HEADS):
        xr = q2[:, nope_w + h * LANES:nope_w + (h + 1) * LANES]
        qcat_ref[h, :, LANES:QCAT] = (_rope(xr, c, s1, s2) * (MLA_SCALE * LOG2E)).astype(BF16)

    o = MLA_Q_LORA
    ckv = _rmsnorm(y[:, o:o + MLA_KV_LORA], kvn_ref[...])
    ckv_ref[...] = ckv
    kcat_ref[:, 0:LANES] = ckv.astype(BF16)
    o += MLA_KV_LORA
    kr = _rope(y[:, o:o + LANES], c, s1, s2)
    kr_ref[...] = kr
    kcat_ref[:, LANES:QCAT] = kr.astype(BF16)
    o += LANES
    sq_ref[...] = (y[:, o:o + SB_WIDTH] * (SB_SCALE * LOG2E)).astype(BF16)
    o += SB_WIDTH
    sk = y[:, o:o + SB_WIDTH]
    sk_ref[...] = sk
    skb_ref[...] = sk.astype(BF16)
    o += SB_WIDTH
    sv = y[:, o:o + SB_WIDTH]
    sv_ref[...] = sv
    svb_ref[...] = sv.astype(BF16)


def _project(x, tabs, n_tab_blocks, w, tm):
    m = x.shape[0]
    full = lambda a: pl.BlockSpec(a.shape, lambda i: (0,) * a.ndim)
    row = lambda n: pl.BlockSpec((tm, n), lambda i: (i, 0))
    tab = pl.BlockSpec((tm, LANES), lambda i: (i % n_tab_blocks, 0))
    out_shape = (
        jax.ShapeDtypeStruct((MLA_HEADS, m, QCAT), BF16),
        jax.ShapeDtypeStruct((m, QCAT), BF16),
        jax.ShapeDtypeStruct((m, MLA_KV_LORA), F32),
        jax.ShapeDtypeStruct((m, LANES), F32),
        jax.ShapeDtypeStruct((m, SB_WIDTH), BF16),
        jax.ShapeDtypeStruct((m, SB_WIDTH), F32),
        jax.ShapeDtypeStruct((m, SB_WIDTH), F32),
        jax.ShapeDtypeStruct((m, SB_WIDTH), BF16),
        jax.ShapeDtypeStruct((m, SB_WIDTH), BF16),
    )
    out_specs = (
        pl.BlockSpec((MLA_HEADS, tm, QCAT), lambda i: (0, i, 0)),
        row(QCAT), row(MLA_KV_LORA), row(LANES), row(SB_WIDTH), row(SB_WIDTH), row(SB_WIDTH),
        row(SB_WIDTH), row(SB_WIDTH),
    )
    ins = (x, w["attn_norm"], w["w_in_p"], w["q_norm"], w["kv_norm"], w["w_uq2"], w["w_uk_bd"]) + tabs
    in_specs = [row(D_MODEL)] + [full(a) for a in ins[1:7]] + [tab, tab, tab]
    return pl.pallas_call(
        _proj_kernel, grid=(m // tm,), in_specs=in_specs, out_specs=out_specs, out_shape=out_shape,
        compiler_params=pltpu.CompilerParams(dimension_semantics=("arbitrary",), vmem_limit_bytes=VMEM_LIMIT),
        name="proj",
    )(*ins)


MLA_ROW_GROUPS = 2


def _mla_kernel(q_ref, k_ref, o_ref, m_s, l_s, acc_s, *, tq, tk):
    i = pl.program_id(1)
    rows = MLA_HEADS * tq
    q = q_ref[...].reshape(rows, QCAT)
    m_s[...] = jnp.full((rows, LANES), MASK_VALUE, F32)
    l_s[...] = jnp.zeros((rows, LANES), F32)
    acc_s[...] = jnp.zeros((rows, LANES), F32)

    gr = rows // MLA_ROW_GROUPS
    groups = [slice(n * gr, (n + 1) * gr) for n in range(MLA_ROW_GROUPS)]

    def block(j, masked):
        k = k_ref[pl.ds(pl.multiple_of(j * tk, tk), tk), :]
        s = [_dot_nt(q[g], k) for g in groups]
        if masked:
            qpos = i * tq + jnp.bitwise_and(lax.broadcasted_iota(jnp.int32, (gr, tk), 0), tq - 1)
            kpos = j * tk + lax.broadcasted_iota(jnp.int32, (gr, tk), 1)
            s = [jnp.where(kpos <= qpos, x, MASK_VALUE) for x in s]
        m_prev = [m_s[g] for g in groups]
        m_next = [jnp.maximum(mp, jnp.max(x, axis=1, keepdims=True)) for mp, x in zip(m_prev, s)]
        alpha = [jnp.exp2(mp - mn) for mp, mn in zip(m_prev, m_next)]
        p = [jnp.concatenate([jnp.exp2(x[:, c * LANES:(c + 1) * LANES] - mn) for c in range(tk // LANES)], axis=1)
             for x, mn in zip(s, m_next)]
        pv = [_dot(x.astype(BF16), k[:, 0:MLA_KV_LORA]) for x in p]
        for n, g in enumerate(groups):
            l_s[g] = alpha[n] * l_s[g] + jnp.sum(p[n], axis=1, keepdims=True)
            acc_s[g] = alpha[n] * acc_s[g] + pv[n]
            m_s[g] = m_next[n]

    n_full = (i * tq) // tk

    def body(j, carry):
        block(j, False)
        return carry

    lax.fori_loop(0, n_full, body, 0)
    block(n_full, True)
    o = acc_s[...] / l_s[...]
    for h in range(MLA_HEADS):
        o_ref[:, h * LANES:(h + 1) * LANES] = o[h * tq:(h + 1) * tq].astype(o_ref.dtype)


def _mla_prompt(qcat, kcat, batch, seq, tq=128, tk=256):
    assert tk % tq == 0 and seq % tk == 0 and tq & (tq - 1) == 0
    m = batch * seq
    nq = seq // tq
    rows = MLA_HEADS * tq
    return pl.pallas_call(
        functools.partial(_mla_kernel, tq=tq, tk=tk),
        grid=(batch, nq),
        in_specs=[pl.BlockSpec((MLA_HEADS, tq, QCAT), lambda b, i: (0, b * nq + i, 0)),
                  pl.BlockSpec((seq, QCAT), lambda b, i: (b, 0))],
        out_specs=pl.BlockSpec((tq, MLA_HEADS * MLA_KV_LORA), lambda b, i: (b * nq + i, 0)),
        out_shape=jax.ShapeDtypeStruct((m, MLA_HEADS * MLA_KV_LORA), BF16),
        scratch_shapes=[pltpu.VMEM((rows, LANES), F32)] * 3,
        compiler_params=pltpu.CompilerParams(dimension_semantics=("arbitrary", "arbitrary"),
                                             vmem_limit_bytes=VMEM_LIMIT),
        name="mla_prompt",
    )(qcat, kcat)


def _sb_kernel(q_ref, k_ref, v_ref, t_ref, o_ref, q2_s, acc_s, carry_s, *, t, pairs):
    i = pl.program_id(2)
    tri = t_ref[...]
    lane = lax.broadcasted_iota(jnp.int32, (t, LANES), 1)
    first = lane < SB_HEAD_DIM
    row = jnp.bitwise_and(lax.broadcasted_iota(jnp.int32, (2 * t, t), 0), t - 1)
    strict = lax.broadcasted_iota(jnp.int32, (2 * t, t), 1) < row
    for p in range(pairs):
        q = q_ref[:, p * LANES:(p + 1) * LANES].astype(F32)
        q2_s[p, 0:t, :] = jnp.where(first, q, 0.0).astype(BF16)
        q2_s[p, t:2 * t, :] = jnp.where(first, 0.0, q).astype(BF16)
    acc_s[...] = jnp.zeros(acc_s.shape, F32)
    carry_s[...] = jnp.zeros(carry_s.shape, F32)

    def block(j, masked):
        start = pl.multiple_of(j * t, t)
        ps = range(pairs)
        z = [_dot_nt(q2_s[p], k_ref[pl.ds(start, t), p * LANES:(p + 1) * LANES]) for p in ps]
        ls, lk = zip(*[_log2_sigmoid_pair(z[p]) for p in ps])
        if masked:
            lk = [jnp.where(strict, lk[p], 0.0) for p in ps]
        split = [_split_bf16(lk[p]) for p in ps]
        after = [_dot(split[p][0], tri) + _dot(split[p][1], tri) for p in ps]
        carry = [carry_s[p] for p in ps]
        log_a = [jnp.concatenate(
            [ls[p][:, c * LANES:(c + 1) * LANES] + after[p][:, c * LANES:(c + 1) * LANES] + carry[p]
             for c in range(t // LANES)], axis=1) for p in ps]
        if masked:
            log_a = [jnp.where(strict, log_a[p], MASK_VALUE) for p in ps]
        a = [jnp.exp2(log_a[p]).astype(BF16) for p in ps]
        pv = [_dot(a[p], v_ref[pl.ds(start, t), p * LANES:(p + 1) * LANES]) for p in ps]
        for p in ps:
            acc_s[p] += pv[p]
            carry_s[p] = carry[p] + jnp.sum(lk[p], axis=1, keepdims=True)

    block(i, True)

    def body(n, c):
        block(i - 1 - n, False)
        return c

    lax.fori_loop(0, i, body, 0)
    for p in range(pairs):
        o_ref[:, p * LANES:(p + 1) * LANES] = jnp.where(first, acc_s[p, 0:t, :], acc_s[p, t:2 * t, :]).astype(o_ref.dtype)


def _sb_prompt(sq, skb, svb, tri, batch, seq, t=256, pairs=4):
    assert seq % t == 0 and t & (t - 1) == 0 and (SB_HEADS // 2) % pairs == 0
    m = batch * seq
    nq = seq // t
    w = pairs * LANES
    return pl.pallas_call(
        functools.partial(_sb_kernel, t=t, pairs=pairs),
        grid=(batch, SB_HEADS // 2 // pairs, nq),
        in_specs=[pl.BlockSpec((t, w), lambda b, p, i: (b * nq + i, p)),
                  pl.BlockSpec((seq, w), lambda b, p, i: (b, p)),
                  pl.BlockSpec((seq, w), lambda b, p, i: (b, p)),
                  pl.BlockSpec((t, t), lambda b, p, i: (0, 0))],
        out_specs=pl.BlockSpec((t, w), lambda b, p, i: (b * nq + i, p)),
        out_shape=jax.ShapeDtypeStruct((m, SB_WIDTH), BF16),
        scratch_shapes=[pltpu.VMEM((pairs, 2 * t, LANES), BF16), pltpu.VMEM((pairs, 2 * t, LANES), F32),
                        pltpu.VMEM((pairs, 2 * t, LANES), F32)],
        compiler_params=pltpu.CompilerParams(dimension_semantics=("arbitrary",) * 3,
                                             vmem_limit_bytes=VMEM_LIMIT),
        name="sb_prompt",
    )(sq, skb, svb, tri)


def _sample_attn_kernel(pt_ref, qcat_ref, knew_ref, sq_ref, t_ref, lat_hbm, kr_hbm, sbk_hbm, sbv_hbm,
                        olat_ref, osb_ref, lat_buf, kr_buf, sbk_buf, sbv_buf, sems, m_s, l_s, acc_s, carry_s, sbacc_s,
                        *, group, n_pages):
    b, g = pl.program_id(0), pl.program_id(1)
    steps = pl.num_programs(1)
    n = b * steps + g
    slot = lax.rem(n, 2)
    pools = ((lat_hbm, lat_buf), (kr_hbm, kr_buf), (sbk_hbm, sbk_buf), (sbv_hbm, sbv_buf))

    def page_copy(idx, page, slot_, r):
        hbm, buf = pools[idx]
        return pltpu.make_async_copy(hbm.at[page], buf.at[slot_, r], sems.at[slot_, idx])

    def start_step(b_, g_, slot_):
        for r in range(group):
            page = pt_ref[b_, n_pages - 1 - (g_ * group + r)]
            for idx in range(len(pools)):
                page_copy(idx, page, slot_, r).start()

    @pl.when(n == 0)
    def _():
        start_step(0, 0, 0)

    @pl.when(n + 1 < pl.num_programs(0) * steps)
    def _():
        wrap = g + 1 == steps
        start_step(jnp.where(wrap, b + 1, b), jnp.where(wrap, 0, g + 1), 1 - slot)

    for r in range(group):
        for idx in range(len(pools)):
            page_copy(idx, 0, slot, r).wait()

    lat_refs = [lat_buf.at[slot, r] for r in range(group)]
    kr_refs = [kr_buf.at[slot, r] for r in range(group)]
    sbk_refs = [sbk_buf.at[slot, r] for r in range(group)]
    sbv_refs = [sbv_buf.at[slot, r] for r in range(group)]
    qc = qcat_ref[...]

    @pl.when(g == 0)
    def _():
        kn = knew_ref[...].astype(F32)
        s_self = jnp.sum(qc.astype(F32) * kn, axis=1, keepdims=True)
        m_s[...] = jnp.broadcast_to(s_self, (MLA_HEADS, LANES))
        l_s[...] = jnp.ones((MLA_HEADS, LANES), F32)
        acc_s[...] = jnp.broadcast_to(kn[:, 0:MLA_KV_LORA], (MLA_HEADS, MLA_KV_LORA))
        carry_s[...] = jnp.zeros((SB_HEADS, LANES), F32)
        sbacc_s[...] = jnp.zeros((SB_HEADS, SB_WIDTH), F32)

    q_lat = qc[:, 0:LANES]
    q_rope = qc[:, LANES:LANES + MLA_ROPE_DIM]
    head_of_lane = lax.broadcasted_iota(jnp.int32, (SB_HEADS, SB_WIDTH), 1) // SB_HEAD_DIM
    own_head = head_of_lane == lax.broadcasted_iota(jnp.int32, (SB_HEADS, SB_WIDTH), 0)
    sq = jnp.broadcast_to(sq_ref[...].astype(F32), (SB_HEADS, SB_WIDTH))
    q_sb = jnp.where(own_head, sq, 0.0).astype(BF16)
    tri = t_ref[...]

    pages = range(group)
    lat = [lat_refs[r][...].astype(BF16) for r in pages]
    s = [_dot_nt(q_lat, lat[r]) + _dot(q_rope, kr_refs[r][...].astype(BF16)) for r in pages]
    m = m_s[...]
    m_next = jnp.maximum(m, jnp.max(functools.reduce(jnp.maximum, s), axis=1, keepdims=True))
    alpha = jnp.exp2(m - m_next)
    p = [jnp.exp2(s[r] - m_next) for r in pages]
    l = alpha * l_s[...] + jnp.sum(functools.reduce(jnp.add, p), axis=1, keepdims=True)
    acc = alpha * acc_s[...] + functools.reduce(jnp.add, [_dot(p[r].astype(BF16), lat[r]) for r in pages])
    m_s[...], l_s[...], acc_s[...] = m_next, l, acc

    z = jnp.concatenate([_dot(q_sb, sbk_refs[r][...].astype(BF16)) for r in pages], axis=0)
    ls, lk = _log2_sigmoid_pair(z)
    hi, lo = _split_bf16(lk)
    after = _dot(hi, tri) + _dot(lo, tri)
    page_sum = jnp.sum(lk, axis=1, keepdims=True)
    log_a = ls + after
    carry = carry_s[...]
    pv = []
    for r in pages:
        rows = slice(r * SB_HEADS, (r + 1) * SB_HEADS)
        a = jnp.exp2(log_a[rows] + carry)
        pv.append(_dot_nt(a.astype(BF16), sbv_refs[r][...].astype(BF16)))
        carry = carry + page_sum[rows]
    sbacc = sbacc_s[...] + functools.reduce(jnp.add, pv)
    carry_s[...], sbacc_s[...] = carry, sbacc

    @pl.when(g == pl.num_programs(1) - 1)
    def _():
        olat_ref[...] = acc / l
        osb_ref[...] = jnp.sum(jnp.where(own_head, sbacc, 0.0), axis=0, keepdims=True)


def _sample_attention(qcat_s, knew, sq_s, tri, lat_pool, kr_pool, sbk_pool, sbv_pool, page_table, group=16):
    db, n_pages = page_table.shape
    assert n_pages % group == 0
    steps = n_pages // group

    per_b = lambda shape: pl.BlockSpec((None,) + shape, lambda b, g, pt: (b, 0, 0))
    pools = (lat_pool, kr_pool, sbk_pool, sbv_pool)
    in_specs = [per_b((MLA_HEADS, QCAT)), per_b((1, QCAT)), per_b((1, SB_WIDTH)),
                pl.BlockSpec((PAGE_SIZE, PAGE_SIZE), lambda b, g, pt: (0, 0))]
    in_specs += [pl.BlockSpec(memory_space=pl.ANY)] * len(pools)
    ring = [pltpu.VMEM((2, group) + pool.shape[1:], pool.dtype) for pool in pools]
    grid_spec = pltpu.PrefetchScalarGridSpec(
        num_scalar_prefetch=1, grid=(db, steps), in_specs=in_specs,
        out_specs=(per_b((MLA_HEADS, MLA_KV_LORA)), per_b((1, SB_WIDTH))),
        scratch_shapes=ring + [pltpu.SemaphoreType.DMA((2, len(pools)))]
        + [pltpu.VMEM((MLA_HEADS, LANES), F32)] * 4 + [pltpu.VMEM((SB_HEADS, SB_WIDTH), F32)])
    operands = [qcat_s, knew, sq_s, tri, *pools]
    return pl.pallas_call(
        functools.partial(_sample_attn_kernel, group=group, n_pages=n_pages),
        grid_spec=grid_spec,
        out_shape=(jax.ShapeDtypeStruct((db, MLA_HEADS, MLA_KV_LORA), F32),
                   jax.ShapeDtypeStruct((db, 1, SB_WIDTH), F32)),
        compiler_params=pltpu.CompilerParams(dimension_semantics=("arbitrary", "arbitrary"),
                                             vmem_limit_bytes=VMEM_LIMIT),
        name="sample_attn",
    )(page_table, *operands)


def _mix_kernel(x_ref, g_ref, wg_ref, olat_ref, wuv_ref, wmo_ref, osb_ref, wso_ref, wout_ref, g2_ref,
                h_ref, hn_ref):
    x = x_ref[...]
    xn = _rmsnorm(x, g_ref[...]).astype(BF16)
    gates = _dot(xn, wg_ref[...])
    olat = olat_ref[...].astype(BF16)
    om = jnp.concatenate([_dot(olat[:, p * QCAT:(p + 1) * QCAT], wuv_ref[p]) for p in range(MLA_HEADS // 2)],
                         axis=1).astype(BF16)
    o_mla = _dot(om, wmo_ref[...])
    o_sb = _dot(osb_ref[...].astype(BF16), wso_ref[...])
    mix = _sigmoid(gates[:, 0:D_MODEL]) * o_mla + _sigmoid(gates[:, D_MODEL:2 * D_MODEL]) * o_sb
    h = x + _dot(mix.astype(BF16), wout_ref[...])
    h_ref[...] = h
    hn_ref[...] = _rmsnorm(h, g2_ref[...]).astype(BF16)


def _mix(x, olat, osb, w, tm):
    m = x.shape[0]
    full = lambda a: pl.BlockSpec(a.shape, lambda i: (0,) * a.ndim)
    row = lambda n: pl.BlockSpec((tm, n), lambda i: (i, 0))
    ins = (x, w["attn_norm"], w["w_gate"], olat, w["w_uv_bd"], w["w_mla_o"], osb, w["w_sb_o"], w["w_out"],
           w["ffn_norm"])
    in_specs = [row(D_MODEL), full(ins[1]), full(ins[2]), row(olat.shape[1]), full(ins[4]), full(ins[5]),
                row(SB_WIDTH), full(ins[7]), full(ins[8]), full(ins[9])]
    return pl.pallas_call(
        _mix_kernel, grid=(m // tm,), in_specs=in_specs,
        out_specs=(row(D_MODEL), row(D_MODEL)),
        out_shape=(jax.ShapeDtypeStruct((m, D_MODEL), F32), jax.ShapeDtypeStruct((m, D_MODEL), BF16)),
        compiler_params=pltpu.CompilerParams(dimension_semantics=("arbitrary",), vmem_limit_bytes=VMEM_LIMIT),
        name="mix",
    )(*ins)


FFN_CHUNK = 256


SUBLANES = 8


def _ffn_chunks(hn, wup_ref, cw_ref, cb_ref, wdn_ref, shifted):
    n_chunks = D_FF // FFN_CHUNK

    def up(c):
        cols = [slice(half * D_FF + c * FFN_CHUNK, half * D_FF + (c + 1) * FFN_CHUNK) for half in range(2)]
        return [(_dot(hn, wup_ref[:, s]), s) for s in cols]

    acc = jnp.zeros((hn.shape[0], D_MODEL), F32)
    ahead = 2
    queue = [up(c) for c in range(ahead)]
    for c in range(n_chunks):
        cur = queue.pop(0)
        if c + ahead < n_chunks:
            queue.append(up(c + ahead))
        conv = []
        for u, cols in cur:
            u1, u2 = shifted(u, cols)
            conv.append(cb_ref[:, cols] + cw_ref[0:1, cols] * u2 + cw_ref[1:2, cols] * u1 + cw_ref[2:3, cols] * u)
        a, b = conv
        gated = (a * _sigmoid(a) * b).astype(BF16)
        acc = acc + _dot(gated, wdn_ref[c * FFN_CHUNK:(c + 1) * FFN_CHUNK, :])
    return acc


def _ffn_out(h_ref, acc, gf_ref, y_ref, final):
    y = h_ref[...] + acc
    y_ref[...] = _rmsnorm(y, gf_ref[...]) if final else y


def _ffn_seq_kernel(hn_ref, h_ref, wup_ref, cw_ref, cb_ref, wdn_ref, gf_ref, y_ref, st_ref, tail_s, *,
                    tm, tiles_per_seq, final):
    @pl.when(pl.program_id(0) % tiles_per_seq == 0)
    def _():
        tail_s[...] = jnp.zeros(tail_s.shape, F32)

    rowid = lax.broadcasted_iota(jnp.int32, (tm, FFN_CHUNK), 0)

    def shifted(u, cols):
        p0, p1 = tail_s[SUBLANES - 2:SUBLANES - 1, cols], tail_s[SUBLANES - 1:SUBLANES, cols]
        u1 = jnp.where(rowid == 0, p1, pltpu.roll(u, 1, 0))
        u2 = jnp.where(rowid == 0, p0, jnp.where(rowid == 1, p1, pltpu.roll(u, 2, 0)))
        tail = u[tm - SUBLANES:tm]
        tail_s[:, cols] = tail
        st_ref[0, :, cols] = tail
        return u1, u2

    acc = _ffn_chunks(hn_ref[...], wup_ref, cw_ref, cb_ref, wdn_ref, shifted)
    _ffn_out(h_ref, acc, gf_ref, y_ref, final)


def _ffn_step_kernel(hn_ref, h_ref, wup_ref, cw_ref, cb_ref, wdn_ref, gf_ref, prev_ref, y_ref, st_ref, *, final):
    def shifted(u, cols):
        later = slice(2 * D_FF + cols.start, 2 * D_FF + cols.stop)
        p0, p1 = prev_ref[:, cols], prev_ref[:, later]
        st_ref[:, cols] = p1
        st_ref[:, later] = u
        return p1, p0

    acc = _ffn_chunks(hn_ref[...], wup_ref, cw_ref, cb_ref, wdn_ref, shifted)
    _ffn_out(h_ref, acc, gf_ref, y_ref, final)


def _ffn(hn, h, prev, w, tm, seq, final):
    m = hn.shape[0]
    full = lambda a: pl.BlockSpec(a.shape, lambda i: (0,) * a.ndim)
    row = lambda n: pl.BlockSpec((tm, n), lambda i: (i, 0))
    weights = (w["w_up"], w["conv_w"], w["conv_b"], w["w_down"], w["final_norm"])
    in_specs = [row(D_MODEL), row(D_MODEL)] + [full(a) for a in weights]
    params = pltpu.CompilerParams(dimension_semantics=("arbitrary",), vmem_limit_bytes=VMEM_LIMIT)
    y_shape = jax.ShapeDtypeStruct((m, D_MODEL), F32)
    if prev is None:
        assert seq % tm == 0 and tm >= SUBLANES
        tiles_per_seq = seq // tm
        y, st = pl.pallas_call(
            functools.partial(_ffn_seq_kernel, tm=tm, tiles_per_seq=tiles_per_seq, final=final),
            grid=(m // tm,), in_specs=in_specs,
            out_specs=(row(D_MODEL), pl.BlockSpec((1, SUBLANES, 2 * D_FF), lambda i: (i // tiles_per_seq, 0, 0))),
            out_shape=(y_shape, jax.ShapeDtypeStruct((m // seq, SUBLANES, 2 * D_FF), F32)),
            scratch_shapes=[pltpu.VMEM((SUBLANES, 2 * D_FF), F32)],
            compiler_params=params, name="ffn_seq",
        )(hn, h, *weights)
        return y, st[:, SUBLANES - (CONV_W - 1):]
    assert seq == 1
    st_w = (CONV_W - 1) * 2 * D_FF
    y, st = pl.pallas_call(
        functools.partial(_ffn_step_kernel, final=final),
        grid=(m // tm,), in_specs=in_specs + [row(st_w)],
        out_specs=(row(D_MODEL), row(st_w)),
        out_shape=(y_shape, jax.ShapeDtypeStruct((m, st_w), F32)),
        compiler_params=params, name="ffn_step",
    )(hn, h, *weights, prev.reshape(m, st_w))
    return y, st.reshape(m, CONV_W - 1, 2 * D_FF)


def _prep_weights(l, attn_norm, w_in, q_norm, kv_norm, w_uq, w_ukv, w_mla_o, w_sb_o, w_out, ffn_norm, w_up, conv_w,
                  conv_b, w_down, final_norm):
    w_in = w_in[l]
    o_kr = MLA_Q_LORA + MLA_KV_LORA
    o_sq = o_kr + MLA_ROPE_DIM
    o_g = o_sq + 3 * SB_WIDTH
    pad = jnp.zeros((D_MODEL, LANES - MLA_ROPE_DIM), w_in.dtype)
    w_in_p = jnp.concatenate([w_in[:, :o_sq], pad, w_in[:, o_sq:o_g]], axis=1).astype(BF16)
    w_gate = w_in[:, o_g:].astype(BF16)
    per_head = MLA_NOPE_DIM + MLA_ROPE_DIM
    uq3 = w_uq[l].reshape(MLA_Q_LORA, MLA_HEADS, per_head)
    uq_nope = uq3[:, :, :MLA_NOPE_DIM].reshape(MLA_Q_LORA, MLA_HEADS * MLA_NOPE_DIM)
    uq_rope = jnp.pad(uq3[:, :, MLA_NOPE_DIM:], ((0, 0), (0, 0), (0, LANES - MLA_ROPE_DIM)))
    w_uq2 = jnp.concatenate([uq_nope, uq_rope.reshape(MLA_Q_LORA, MLA_HEADS * LANES)], axis=1).astype(BF16)
    ukv3 = w_ukv[l].reshape(MLA_KV_LORA, MLA_HEADS, MLA_NOPE_DIM + MLA_V_DIM)
    uk_t = jnp.transpose(ukv3[:, :, :MLA_NOPE_DIM], (1, 2, 0))
    uv = jnp.transpose(ukv3[:, :, MLA_NOPE_DIM:], (1, 0, 2))
    zk = jnp.zeros_like(uk_t[0])
    zv = jnp.zeros_like(uv[0])
    w_uk_bd = jnp.stack([jnp.block([[uk_t[2 * p], zk], [zk, uk_t[2 * p + 1]]]) for p in range(MLA_HEADS // 2)])
    w_uv_bd = jnp.stack([jnp.block([[uv[2 * p], zv], [zv, uv[2 * p + 1]]]) for p in range(MLA_HEADS // 2)])
    return {
        "attn_norm": attn_norm[l][None, :], "w_in_p": w_in_p, "w_gate": w_gate,
        "q_norm": q_norm[l][None, :], "kv_norm": kv_norm[l][None, :], "w_uq2": w_uq2,
        "w_uk_bd": w_uk_bd.astype(BF16), "w_uv_bd": w_uv_bd.astype(BF16),
        "w_mla_o": w_mla_o[l].astype(BF16), "w_sb_o": w_sb_o[l].astype(BF16), "w_out": w_out[l].astype(BF16),
        "ffn_norm": ffn_norm[l][None, :], "w_up": w_up[l].astype(BF16), "conv_w": conv_w[l],
        "conv_b": conv_b[l][None, :], "w_down": w_down[l].astype(BF16), "final_norm": final_norm[None, :],
    }


def _rope_tables(pos):
    inv = ROPE_BASE ** (-jnp.arange(ROPE_HALF, dtype=F32) / ROPE_HALF)
    ang = pos.astype(F32)[:, None] * inv[None, :]
    cos, sin = jnp.cos(ang), jnp.sin(ang)
    z = jnp.zeros_like(cos)
    zpad = jnp.zeros((pos.shape[0], LANES - MLA_ROPE_DIM), F32)
    return (jnp.concatenate([cos, cos, zpad], axis=1), jnp.concatenate([z, sin, zpad], axis=1),
            jnp.concatenate([-sin, z, zpad], axis=1))


def _tri(n):
    r = lax.broadcasted_iota(jnp.int32, (n, n), 0)
    c = lax.broadcasted_iota(jnp.int32, (n, n), 1)
    return (r > c).astype(BF16)


def _pick_tile(n, target):
    t = min(n, target)
    assert n % t == 0
    return t


def kernel(x_prompt, x_sample, cache_latent, cache_krope, cache_sb_k, cache_sb_v, state_conv, page_table, attn_norm, w_in, q_norm, kv_norm, w_uq, w_ukv, w_mla_o, w_sb_o, w_out, ffn_norm, w_up, conv_w, conv_b, w_down, final_norm):
    batch, seq, _ = x_prompt.shape
    db, dec_seq, _ = x_sample.shape
    assert dec_seq == 1
    depth = w_in.shape[0]
    n_pool = cache_latent.shape[1]
    n_pages = page_table.shape[1]
    mp, ms = batch * seq, db * dec_seq
    tabs_p = _rope_tables(jnp.arange(seq))
    tabs_s = _rope_tables(jnp.full((ms,), n_pages * PAGE_SIZE, jnp.int32))
    sb_t = _pick_tile(seq, 256)
    tri_p, tri_s = _tri(sb_t), _tri(PAGE_SIZE)
    tm_p = _pick_tile(seq, 512)
    tm_f = _pick_tile(seq, 512)
    tm_s = _pick_tile(ms, 128)

    def keys_minor(pool):
        return jnp.transpose(pool, (0, 2, 3, 1)).reshape(n_pool, SB_WIDTH, PAGE_SIZE)

    hp = x_prompt.reshape(mp, D_MODEL)
    hs = x_sample.reshape(ms, D_MODEL)
    outs = [[] for _ in range(10)]
    for l in range(depth):
        final = l == depth - 1
        w = _prep_weights(l, attn_norm, w_in, q_norm, kv_norm, w_uq, w_ukv, w_mla_o, w_sb_o, w_out, ffn_norm, w_up,
                          conv_w, conv_b, w_down, final_norm)
        qcat, kcat, ckv, kr, sq, sk, sv, _, _ = _project(hs, tabs_s, 1, w, tm_s)
        olat, osb = _sample_attention(
            jnp.swapaxes(qcat, 0, 1), jnp.concatenate([ckv, kr], axis=1).reshape(ms, 1, QCAT),
            sq.reshape(ms, 1, SB_WIDTH), tri_s,
            cache_latent[l], jnp.swapaxes(cache_krope[l], 1, 2), keys_minor(cache_sb_k[l]), keys_minor(cache_sb_v[l]),
            page_table)
        h, hn = _mix(hs, olat.reshape(ms, MLA_HEADS * MLA_KV_LORA), osb.reshape(ms, SB_WIDTH), w, tm_s)
        hs, conv_st = _ffn(hn, h, state_conv[l], w, tm_s, 1, final)
        outs[5].append(ckv.reshape(db, dec_seq, MLA_KV_LORA))
        outs[6].append(kr[:, :MLA_ROPE_DIM].reshape(db, dec_seq, MLA_ROPE_DIM))
        outs[7].append(sk.reshape(db, dec_seq, SB_HEADS, SB_HEAD_DIM))
        outs[8].append(sv.reshape(db, dec_seq, SB_HEADS, SB_HEAD_DIM))
        outs[9].append(conv_st)
        qcat, kcat, ckv, kr, sq, sk, sv, skb, svb = _project(hp, tabs_p, seq // tm_p, w, tm_p)
        olat = _mla_prompt(qcat, kcat, batch, seq, tq=_pick_tile(seq, 256), tk=_pick_tile(seq, 256))
        osb = _sb_prompt(sq, skb, svb, tri_p, batch, seq, t=sb_t)
        h, hn = _mix(hp, olat, osb, w, tm_p)
        hp, conv_st = _ffn(hn, h, None, w, tm_f, seq, final)
        outs[0].append(ckv.reshape(batch, seq, MLA_KV_LORA))
        outs[1].append(kr[:, :MLA_ROPE_DIM].reshape(batch, seq, MLA_ROPE_DIM))
        outs[2].append(sk.reshape(batch, seq, SB_HEADS, SB_HEAD_DIM))
        outs[3].append(sv.reshape(batch, seq, SB_HEADS, SB_HEAD_DIM))
        outs[4].append(conv_st)
    return (hp.reshape(batch, seq, D_MODEL), hs.reshape(db, dec_seq, D_MODEL)) + tuple(jnp.stack(o) for o in outs)
```

```python
import functools

import jax
import jax.numpy as jnp
from jax import lax
from jax.experimental import pallas as pl
from jax.experimental.pallas import tpu as pltpu

D_MODEL = 1024
PAGE_SIZE = 128
MLA_HEADS = 8
MLA_NOPE_DIM = 64
MLA_ROPE_DIM = 32
MLA_V_DIM = 64
MLA_Q_LORA = 256
MLA_KV_LORA = 128
ROPE_BASE = 10000.0
MLA_SCALE = (MLA_NOPE_DIM + MLA_ROPE_DIM) ** -0.5
SB_HEADS = 8
SB_HEAD_DIM = 64
SB_WIDTH = SB_HEADS * SB_HEAD_DIM
SB_SCALE = SB_HEAD_DIM ** -0.5
D_FF = 2816
CONV_W = 3
NORM_EPS = 1e-6

LANES = 128
QCAT = 2 * LANES
ROPE_HALF = MLA_ROPE_DIM // 2
MASK_VALUE = -1e30
VMEM_LIMIT = 56 * 1024 * 1024

BF16 = jnp.bfloat16
F32 = jnp.float32

_NT = (((1,), (1,)), ((), ()))


def _dot(a, b):
    return jnp.dot(a, b, preferred_element_type=F32)


def _dot_nt(a, b):
    return lax.dot_general(a, b, _NT, preferred_element_type=F32)


def _rmsnorm(x, g):
    return x * lax.rsqrt(jnp.mean(x * x, axis=-1, keepdims=True) + NORM_EPS) * g


def _sigmoid(x):
    return 1.0 / (1.0 + jnp.exp(-x))


def _rope(x, c, s1, s2):
    return x * c + pltpu.roll(x, ROPE_HALF, 1) * s1 + pltpu.roll(x, LANES - ROPE_HALF, 1) * s2


LOG2E = 1.4426950408889634


def _log2_sigmoid_pair(z2):
    soft = jnp.log2(1.0 + jnp.exp2(-jnp.abs(z2)))
    ls = jnp.minimum(z2, 0.0) - soft
    return ls, ls - z2


def _split_bf16(x):
    hi = x.astype(BF16)
    lo = (x - hi.astype(F32)).astype(BF16)
    return hi, lo


def _proj_kernel(x_ref, g_ref, w_ref, wkvt_ref, qn_ref, kvn_ref, wuq_ref, wuk_ref, c_ref, s1_ref, s2_ref,
                 qcat_ref, kcat_ref, ckv_ref, kr_ref, sq_ref, skt_ref, svt_ref, skb_ref, svb_ref):
    xn = _rmsnorm(x_ref[...], g_ref[...]).astype(BF16)
    y = _dot(xn, w_ref[...])
    kvt = _dot_nt(wkvt_ref[...], xn)
    skt_ref[...] = kvt[0:SB_WIDTH]
    svt_ref[...] = kvt[SB_WIDTH:2 * SB_WIDTH]
    c, s1, s2 = c_ref[...], s1_ref[...], s2_ref[...]

    cqn = _rmsnorm(y[:, 0:MLA_Q_LORA], qn_ref[...]).astype(BF16)
    q2 = _dot(cqn, wuq_ref[...])
    nope_w = MLA_HEADS * MLA_NOPE_DIM
    for p in range(MLA_HEADS // 2):
        ql = _dot(q2[:, p * LANES:(p + 1) * LANES].astype(BF16), wuk_ref[p]) * (MLA_SCALE * LOG2E)
        for e in range(2):
            qcat_ref[2 * p + e, :, 0:LANES] = ql[:, e * LANES:(e + 1) * LANES].astype(BF16)
    for h in range(MLA_HEADS):
        xr = q2[:, nope_w + h * LANES:nope_w + (h + 1) * LANES]
        qcat_ref[h, :, LANES:QCAT] = (_rope(xr, c, s1, s2) * (MLA_SCALE * LOG2E)).astype(BF16)

    o = MLA_Q_LORA
    ckv = _rmsnorm(y[:, o:o + MLA_KV_LORA], kvn_ref[...])
    ckv_ref[...] = ckv
    kcat_ref[:, 0:LANES] = ckv.astype(BF16)
    o += MLA_KV_LORA
    kr = _rope(y[:, o:o + LANES], c, s1, s2)
    kr_ref[...] = kr
    kcat_ref[:, LANES:QCAT] = kr.astype(BF16)
    o += LANES
    sq_ref[...] = (y[:, o:o + SB_WIDTH] * (SB_SCALE * LOG2E)).astype(BF16)
    o += SB_WIDTH
    skb_ref[...] = y[:, o:o + SB_WIDTH].astype(BF16)
    o += SB_WIDTH
    svb_ref[...] = y[:, o:o + SB_WIDTH].astype(BF16)


def _project(x, tabs, tiles_per_seq, w, tm):
    m = x.shape[0]
    n_seq = m // (tiles_per_seq * tm)
    full = lambda a: pl.BlockSpec(a.shape, lambda i: (0,) * a.ndim)
    row = lambda n: pl.BlockSpec((tm, n), lambda i: (i, 0))
    tab = pl.BlockSpec((tm, LANES), lambda i: (i % tiles_per_seq, 0))
    feat_major = pl.BlockSpec((None, SB_WIDTH, tm), lambda i: (i // tiles_per_seq, 0, i % tiles_per_seq))
    feat_major_shape = jax.ShapeDtypeStruct((n_seq, SB_WIDTH, tiles_per_seq * tm), F32)
    out_shape = (
        jax.ShapeDtypeStruct((MLA_HEADS, m, QCAT), BF16),
        jax.ShapeDtypeStruct((m, QCAT), BF16),
        jax.ShapeDtypeStruct((m, MLA_KV_LORA), F32),
        jax.ShapeDtypeStruct((m, LANES), F32),
        jax.ShapeDtypeStruct((m, SB_WIDTH), BF16),
        feat_major_shape,
        feat_major_shape,
        jax.ShapeDtypeStruct((m, SB_WIDTH), BF16),
        jax.ShapeDtypeStruct((m, SB_WIDTH), BF16),
    )
    out_specs = (
        pl.BlockSpec((MLA_HEADS, tm, QCAT), lambda i: (0, i, 0)),
        row(QCAT), row(MLA_KV_LORA), row(LANES), row(SB_WIDTH), feat_major, feat_major,
        row(SB_WIDTH), row(SB_WIDTH),
    )
    ins = (x, w["attn_norm"], w["w_in_p"], w["w_kv_t"], w["q_norm"], w["kv_norm"], w["w_uq2"], w["w_uk_bd"]) + tabs
    in_specs = [row(D_MODEL)] + [full(a) for a in ins[1:8]] + [tab, tab, tab]
    return pl.pallas_call(
        _proj_kernel, grid=(m // tm,), in_specs=in_specs, out_specs=out_specs, out_shape=out_shape,
        compiler_params=pltpu.CompilerParams(dimension_semantics=("arbitrary",), vmem_limit_bytes=VMEM_LIMIT),
        name="proj",
    )(*ins)


MLA_ROW_GROUPS = 2


def _mla_kernel(q_ref, k_ref, o_ref, m_s, l_s, acc_s, *, tq, tk):
    i = pl.program_id(1)
    rows = MLA_HEADS * tq
    q = q_ref[...].reshape(rows, QCAT)
    m_s[...] = jnp.full((rows, LANES), MASK_VALUE, F32)
    l_s[...] = jnp.zeros((rows, LANES), F32)
    acc_s[...] = jnp.zeros((rows, LANES), F32)

    gr = rows // MLA_ROW_GROUPS
    groups = [slice(n * gr, (n + 1) * gr) for n in range(MLA_ROW_GROUPS)]

    def block(j, masked):
        k = k_ref[pl.ds(pl.multiple_of(j * tk, tk), tk), :]
        s = [_dot_nt(q[g], k) for g in groups]
        if masked:
            qpos = i * tq + jnp.bitwise_and(lax.broadcasted_iota(jnp.int32, (gr, tk), 0), tq - 1)
            kpos = j * tk + lax.broadcasted_iota(jnp.int32, (gr, tk), 1)
            s = [jnp.where(kpos <= qpos, x, MASK_VALUE) for x in s]
        m_prev = [m_s[g] for g in groups]
        m_next = [jnp.maximum(mp, jnp.max(x, axis=1, keepdims=True)) for mp, x in zip(m_prev, s)]
        alpha = [jnp.exp2(mp - mn) for mp, mn in zip(m_prev, m_next)]
        p = [jnp.concatenate([jnp.exp2(x[:, c * LANES:(c + 1) * LANES] - mn) for c in range(tk // LANES)], axis=1)
             for x, mn in zip(s, m_next)]
        pv = [_dot(x.astype(BF16), k[:, 0:MLA_KV_LORA]) for x in p]
        for n, g in enumerate(groups):
            l_s[g] = alpha[n] * l_s[g] + jnp.sum(p[n], axis=1, keepdims=True)
            acc_s[g] = alpha[n] * acc_s[g] + pv[n]
            m_s[g] = m_next[n]

    n_full = (i * tq) // tk

    def body(j, carry):
        block(j, False)
        return carry

    lax.fori_loop(0, n_full, body, 0)
    block(n_full, True)
    o = acc_s[...] / l_s[...]
    for h in range(MLA_HEADS):
        o_ref[:, h * LANES:(h + 1) * LANES] = o[h * tq:(h + 1) * tq].astype(o_ref.dtype)


def _mla_prompt(qcat, kcat, batch, seq, tq=128, tk=256):
    assert tk % tq == 0 and seq % tk == 0 and tq & (tq - 1) == 0
    m = batch * seq
    nq = seq // tq
    rows = MLA_HEADS * tq
    return pl.pallas_call(
        functools.partial(_mla_kernel, tq=tq, tk=tk),
        grid=(batch, nq),
        in_specs=[pl.BlockSpec((MLA_HEADS, tq, QCAT), lambda b, i: (0, b * nq + i, 0)),
                  pl.BlockSpec((seq, QCAT), lambda b, i: (b, 0))],
        out_specs=pl.BlockSpec((tq, MLA_HEADS * MLA_KV_LORA), lambda b, i: (b * nq + i, 0)),
        out_shape=jax.ShapeDtypeStruct((m, MLA_HEADS * MLA_KV_LORA), BF16),
        scratch_shapes=[pltpu.VMEM((rows, LANES), F32)] * 3,
        compiler_params=pltpu.CompilerParams(dimension_semantics=("arbitrary", "arbitrary"),
                                             vmem_limit_bytes=VMEM_LIMIT),
        name="mla_prompt",
    )(qcat, kcat)


def _sb_kernel(q_ref, k_ref, v_ref, t_ref, o_ref, q2_s, acc_s, carry_s, *, t, pairs):
    i = pl.program_id(2)
    tri = t_ref[...]
    lane = lax.broadcasted_iota(jnp.int32, (t, LANES), 1)
    first = lane < SB_HEAD_DIM
    row = jnp.bitwise_and(lax.broadcasted_iota(jnp.int32, (2 * t, t), 0), t - 1)
    strict = lax.broadcasted_iota(jnp.int32, (2 * t, t), 1) < row
    for p in range(pairs):
        q = q_ref[:, p * LANES:(p + 1) * LANES].astype(F32)
        q2_s[p, 0:t, :] = jnp.where(first, q, 0.0).astype(BF16)
        q2_s[p, t:2 * t, :] = jnp.where(first, 0.0, q).astype(BF16)
    acc_s[...] = jnp.zeros(acc_s.shape, F32)
    carry_s[...] = jnp.zeros(carry_s.shape, F32)

    def block(j, masked):
        start = pl.multiple_of(j * t, t)
        ps = range(pairs)
        z = [_dot_nt(q2_s[p], k_ref[pl.ds(start, t), p * LANES:(p + 1) * LANES]) for p in ps]
        ls, lk = zip(*[_log2_sigmoid_pair(z[p]) for p in ps])
        if masked:
            lk = [jnp.where(strict, lk[p], 0.0) for p in ps]
        split = [_split_bf16(lk[p]) for p in ps]
        after = [_dot(split[p][0], tri) + _dot(split[p][1], tri) for p in ps]
        carry = [carry_s[p] for p in ps]
        log_a = [jnp.concatenate(
            [ls[p][:, c * LANES:(c + 1) * LANES] + after[p][:, c * LANES:(c + 1) * LANES] + carry[p]
             for c in range(t // LANES)], axis=1) for p in ps]
        if masked:
            log_a = [jnp.where(strict, log_a[p], MASK_VALUE) for p in ps]
        a = [jnp.exp2(log_a[p]).astype(BF16) for p in ps]
        pv = [_dot(a[p], v_ref[pl.ds(start, t), p * LANES:(p + 1) * LANES]) for p in ps]
        for p in ps:
            acc_s[p] += pv[p]
            carry_s[p] = carry[p] + jnp.sum(lk[p], axis=1, keepdims=True)

    block(i, True)

    def body(n, c):
        block(i - 1 - n, False)
        return c

    lax.fori_loop(0, i, body, 0)
    for p in range(pairs):
        o_ref[:, p * LANES:(p + 1) * LANES] = jnp.where(first, acc_s[p, 0:t, :], acc_s[p, t:2 * t, :]).astype(o_ref.dtype)


def _sb_prompt(sq, skb, svb, tri, batch, seq, t=256, pairs=4):
    assert seq % t == 0 and t & (t - 1) == 0 and (SB_HEADS // 2) % pairs == 0
    m = batch * seq
    nq = seq // t
    w = pairs * LANES
    return pl.pallas_call(
        functools.partial(_sb_kernel, t=t, pairs=pairs),
        grid=(batch, SB_HEADS // 2 // pairs, nq),
        in_specs=[pl.BlockSpec((t, w), lambda b, p, i: (b * nq + i, p)),
                  pl.BlockSpec((seq, w), lambda b, p, i: (b, p)),
                  pl.BlockSpec((seq, w), lambda b, p, i: (b, p)),
                  pl.BlockSpec((t, t), lambda b, p, i: (0, 0))],
        out_specs=pl.BlockSpec((t, w), lambda b, p, i: (b * nq + i, p)),
        out_shape=jax.ShapeDtypeStruct((m, SB_WIDTH), BF16),
        scratch_shapes=[pltpu.VMEM((pairs, 2 * t, LANES), BF16), pltpu.VMEM((pairs, 2 * t, LANES), F32),
                        pltpu.VMEM((pairs, 2 * t, LANES), F32)],
        compiler_params=pltpu.CompilerParams(dimension_semantics=("arbitrary",) * 3,
                                             vmem_limit_bytes=VMEM_LIMIT),
        name="sb_prompt",
    )(sq, skb, svb, tri)


def _sample_attn_kernel(pt_ref, qcat_ref, knew_ref, sq_ref, t_ref, lat_hbm, kr_hbm, sbk_hbm, sbv_hbm,
                        olat_ref, osb_ref, lat_buf, kr_buf, sbk_buf, sbv_buf, sems, m_s, l_s, acc_s, carry_s, sbacc_s,
                        *, group, n_pages):
    b, g = pl.program_id(0), pl.program_id(1)
    steps = pl.num_programs(1)
    n = b * steps + g
    slot = lax.rem(n, 2)
    pools = ((lat_hbm, lat_buf), (kr_hbm, kr_buf), (sbk_hbm, sbk_buf), (sbv_hbm, sbv_buf))

    def page_copy(idx, page, slot_, r):
        hbm, buf = pools[idx]
        return pltpu.make_async_copy(hbm.at[page], buf.at[slot_, r], sems.at[slot_, idx])

    def start_step(b_, g_, slot_):
        for r in range(group):
            page = pt_ref[b_, n_pages - 1 - (g_ * group + r)]
            for idx in range(len(pools)):
                page_copy(idx, page, slot_, r).start()

    @pl.when(n == 0)
    def _():
        start_step(0, 0, 0)

    @pl.when(n + 1 < pl.num_programs(0) * steps)
    def _():
        wrap = g + 1 == steps
        start_step(jnp.where(wrap, b + 1, b), jnp.where(wrap, 0, g + 1), 1 - slot)

    for r in range(group):
        for idx in range(len(pools)):
            page_copy(idx, 0, slot, r).wait()

    lat_refs = [lat_buf.at[slot, r] for r in range(group)]
    kr_refs = [kr_buf.at[slot, r] for r in range(group)]
    sbk_refs = [sbk_buf.at[slot, r] for r in range(group)]
    sbv_refs = [sbv_buf.at[slot, r] for r in range(group)]
    qc = qcat_ref[...]

    @pl.when(g == 0)
    def _():
        kn = knew_ref[...].astype(F32)
        s_self = jnp.sum(qc.astype(F32) * kn, axis=1, keepdims=True)
        m_s[...] = jnp.broadcast_to(s_self, (MLA_HEADS, LANES))
        l_s[...] = jnp.ones((MLA_HEADS, LANES), F32)
        acc_s[...] = jnp.broadcast_to(kn[:, 0:MLA_KV_LORA], (MLA_HEADS, MLA_KV_LORA))
        carry_s[...] = jnp.zeros((SB_HEADS, LANES), F32)
        sbacc_s[...] = jnp.zeros((SB_HEADS, SB_WIDTH), F32)

    q_lat = qc[:, 0:LANES]
    q_rope = qc[:, LANES:LANES + MLA_ROPE_DIM]
    head_of_lane = lax.broadcasted_iota(jnp.int32, (SB_HEADS, SB_WIDTH), 1) // SB_HEAD_DIM
    own_head = head_of_lane == lax.broadcasted_iota(jnp.int32, (SB_HEADS, SB_WIDTH), 0)
    sq = jnp.broadcast_to(sq_ref[...].astype(F32), (SB_HEADS, SB_WIDTH))
    q_sb = jnp.where(own_head, sq, 0.0).astype(BF16)
    tri = t_ref[...]

    pages = range(group)
    lat = [lat_refs[r][...].astype(BF16) for r in pages]
    s = [_dot_nt(q_lat, lat[r]) + _dot(q_rope, kr_refs[r][...].astype(BF16)) for r in pages]
    m = m_s[...]
    m_next = jnp.maximum(m, jnp.max(functools.reduce(jnp.maximum, s), axis=1, keepdims=True))
    alpha = jnp.exp2(m - m_next)
    p = [jnp.exp2(s[r] - m_next) for r in pages]
    l = alpha * l_s[...] + jnp.sum(functools.reduce(jnp.add, p), axis=1, keepdims=True)
    acc = alpha * acc_s[...] + functools.reduce(jnp.add, [_dot(p[r].astype(BF16), lat[r]) for r in pages])
    m_s[...], l_s[...], acc_s[...] = m_next, l, acc

    z = jnp.concatenate([_dot(q_sb, sbk_refs[r][...].astype(BF16)) for r in pages], axis=0)
    ls, lk = _log2_sigmoid_pair(z)
    hi, lo = _split_bf16(lk)
    after = _dot(hi, tri) + _dot(lo, tri)
    page_sum = jnp.sum(lk, axis=1, keepdims=True)
    log_a = ls + after
    carry = carry_s[...]
    pv = []
    for r in pages:
        rows = slice(r * SB_HEADS, (r + 1) * SB_HEADS)
        a = jnp.exp2(log_a[rows] + carry)
        pv.append(_dot_nt(a.astype(BF16), sbv_refs[r][...].astype(BF16)))
        carry = carry + page_sum[rows]
    sbacc = sbacc_s[...] + functools.reduce(jnp.add, pv)
    carry_s[...], sbacc_s[...] = carry, sbacc

    @pl.when(g == pl.num_programs(1) - 1)
    def _():
        olat_ref[...] = acc / l
        osb_ref[...] = jnp.sum(jnp.where(own_head, sbacc, 0.0), axis=0, keepdims=True)


def _sample_attention(qcat_s, knew, sq_s, tri, lat_pool, kr_pool, sbk_pool, sbv_pool, page_table, group=16):
    db, n_pages = page_table.shape
    assert n_pages % group == 0
    steps = n_pages // group

    per_b = lambda shape: pl.BlockSpec((None,) + shape, lambda b, g, pt: (b, 0, 0))
    pools = (lat_pool, kr_pool, sbk_pool, sbv_pool)
    in_specs = [per_b((MLA_HEADS, QCAT)), per_b((1, QCAT)), per_b((1, SB_WIDTH)),
                pl.BlockSpec((PAGE_SIZE, PAGE_SIZE), lambda b, g, pt: (0, 0))]
    in_specs += [pl.BlockSpec(memory_space=pl.ANY)] * len(pools)
    ring = [pltpu.VMEM((2, group) + pool.shape[1:], pool.dtype) for pool in pools]
    grid_spec = pltpu.PrefetchScalarGridSpec(
        num_scalar_prefetch=1, grid=(db, steps), in_specs=in_specs,
        out_specs=(per_b((MLA_HEADS, MLA_KV_LORA)), per_b((1, SB_WIDTH))),
        scratch_shapes=ring + [pltpu.SemaphoreType.DMA((2, len(pools)))]
        + [pltpu.VMEM((MLA_HEADS, LANES), F32)] * 4 + [pltpu.VMEM((SB_HEADS, SB_WIDTH), F32)])
    operands = [qcat_s, knew, sq_s, tri, *pools]
    return pl.pallas_call(
        functools.partial(_sample_attn_kernel, group=group, n_pages=n_pages),
        grid_spec=grid_spec,
        out_shape=(jax.ShapeDtypeStruct((db, MLA_HEADS, MLA_KV_LORA), F32),
                   jax.ShapeDtypeStruct((db, 1, SB_WIDTH), F32)),
        compiler_params=pltpu.CompilerParams(dimension_semantics=("arbitrary", "arbitrary"),
                                             vmem_limit_bytes=VMEM_LIMIT),
        name="sample_attn",
    )(page_table, *operands)


def _mix_kernel(x_ref, g_ref, wg_ref, olat_ref, wuv_ref, wmo_ref, osb_ref, wso_ref, wout_ref, g2_ref,
                h_ref, hn_ref):
    x = x_ref[...]
    xn = _rmsnorm(x, g_ref[...]).astype(BF16)
    gates = _dot(xn, wg_ref[...])
    olat = olat_ref[...].astype(BF16)
    om = jnp.concatenate([_dot(olat[:, p * QCAT:(p + 1) * QCAT], wuv_ref[p]) for p in range(MLA_HEADS // 2)],
                         axis=1).astype(BF16)
    o_mla = _dot(om, wmo_ref[...])
    o_sb = _dot(osb_ref[...].astype(BF16), wso_ref[...])
    mix = _sigmoid(gates[:, 0:D_MODEL]) * o_mla + _sigmoid(gates[:, D_MODEL:2 * D_MODEL]) * o_sb
    h = x + _dot(mix.astype(BF16), wout_ref[...])
    h_ref[...] = h
    hn_ref[...] = _rmsnorm(h, g2_ref[...]).astype(BF16)


def _mix(x, olat, osb, w, tm):
    m = x.shape[0]
    full = lambda a: pl.BlockSpec(a.shape, lambda i: (0,) * a.ndim)
    row = lambda n: pl.BlockSpec((tm, n), lambda i: (i, 0))
    ins = (x, w["attn_norm"], w["w_gate"], olat, w["w_uv_bd"], w["w_mla_o"], osb, w["w_sb_o"], w["w_out"],
           w["ffn_norm"])
    in_specs = [row(D_MODEL), full(ins[1]), full(ins[2]), row(olat.shape[1]), full(ins[4]), full(ins[5]),
                row(SB_WIDTH), full(ins[7]), full(ins[8]), full(ins[9])]
    return pl.pallas_call(
        _mix_kernel, grid=(m // tm,), in_specs=in_specs,
        out_specs=(row(D_MODEL), row(D_MODEL)),
        out_shape=(jax.ShapeDtypeStruct((m, D_MODEL), F32), jax.ShapeDtypeStruct((m, D_MODEL), BF16)),
        compiler_params=pltpu.CompilerParams(dimension_semantics=("arbitrary",), vmem_limit_bytes=VMEM_LIMIT),
        name="mix",
    )(*ins)


FFN_CHUNK = 256


SUBLANES = 8


def _ffn_chunks(hn, wup_ref, cw_ref, cb_ref, wdn_ref, shifted):
    n_chunks = D_FF // FFN_CHUNK

    def up(c):
        cols = [slice(half * D_FF + c * FFN_CHUNK, half * D_FF + (c + 1) * FFN_CHUNK) for half in range(2)]
        return [(_dot(hn, wup_ref[:, s]), s) for s in cols]

    acc = jnp.zeros((hn.shape[0], D_MODEL), F32)
    ahead = 2
    queue = [up(c) for c in range(ahead)]
    for c in range(n_chunks):
        cur = queue.pop(0)
        if c + ahead < n_chunks:
            queue.append(up(c + ahead))
        conv = []
        for u, cols in cur:
            u1, u2 = shifted(u, cols)
            conv.append(cb_ref[:, cols] + cw_ref[0:1, cols] * u2 + cw_ref[1:2, cols] * u1 + cw_ref[2:3, cols] * u)
        a, b = conv
        gated = (a * _sigmoid(a) * b).astype(BF16)
        acc = acc + _dot(gated, wdn_ref[c * FFN_CHUNK:(c + 1) * FFN_CHUNK, :])
    return acc


def _ffn_out(h_ref, acc, gf_ref, y_ref, final):
    y = h_ref[...] + acc
    y_ref[...] = _rmsnorm(y, gf_ref[...]) if final else y


def _ffn_seq_kernel(hn_ref, h_ref, wup_ref, cw_ref, cb_ref, wdn_ref, gf_ref, y_ref, st_ref, tail_s, *,
                    tm, tiles_per_seq, final):
    @pl.when(pl.program_id(0) % tiles_per_seq == 0)
    def _():
        tail_s[...] = jnp.zeros(tail_s.shape, F32)

    rowid = lax.broadcasted_iota(jnp.int32, (tm, FFN_CHUNK), 0)

    def shifted(u, cols):
        p0, p1 = tail_s[SUBLANES - 2:SUBLANES - 1, cols], tail_s[SUBLANES - 1:SUBLANES, cols]
        u1 = jnp.where(rowid == 0, p1, pltpu.roll(u, 1, 0))
        u2 = jnp.where(rowid == 0, p0, jnp.where(rowid == 1, p1, pltpu.roll(u, 2, 0)))
        tail = u[tm - SUBLANES:tm]
        tail_s[:, cols] = tail
        st_ref[0, :, cols] = tail
        return u1, u2

    acc = _ffn_chunks(hn_ref[...], wup_ref, cw_ref, cb_ref, wdn_ref, shifted)
    _ffn_out(h_ref, acc, gf_ref, y_ref, final)


def _ffn_step_kernel(hn_ref, h_ref, wup_ref, cw_ref, cb_ref, wdn_ref, gf_ref, prev_ref, y_ref, st_ref, *, final):
    def shifted(u, cols):
        later = slice(2 * D_FF + cols.start, 2 * D_FF + cols.stop)
        p0, p1 = prev_ref[:, cols], prev_ref[:, later]
        st_ref[:, cols] = p1
        st_ref[:, later] = u
        return p1, p0

    acc = _ffn_chunks(hn_ref[...], wup_ref, cw_ref, cb_ref, wdn_ref, shifted)
    _ffn_out(h_ref, acc, gf_ref, y_ref, final)


def _ffn(hn, h, prev, w, tm, seq, final):
    m = hn.shape[0]
    full = lambda a: pl.BlockSpec(a.shape, lambda i: (0,) * a.ndim)
    row = lambda n: pl.BlockSpec((tm, n), lambda i: (i, 0))
    weights = (w["w_up"], w["conv_w"], w["conv_b"], w["w_down"], w["final_norm"])
    in_specs = [row(D_MODEL), row(D_MODEL)] + [full(a) for a in weights]
    params = pltpu.CompilerParams(dimension_semantics=("arbitrary",), vmem_limit_bytes=VMEM_LIMIT)
    y_shape = jax.ShapeDtypeStruct((m, D_MODEL), F32)
    if prev is None:
        assert seq % tm == 0 and tm >= SUBLANES
        tiles_per_seq = seq // tm
        y, st = pl.pallas_call(
            functools.partial(_ffn_seq_kernel, tm=tm, tiles_per_seq=tiles_per_seq, final=final),
            grid=(m // tm,), in_specs=in_specs,
            out_specs=(row(D_MODEL), pl.BlockSpec((1, SUBLANES, 2 * D_FF), lambda i: (i // tiles_per_seq, 0, 0))),
            out_shape=(y_shape, jax.ShapeDtypeStruct((m // seq, SUBLANES, 2 * D_FF), F32)),
            scratch_shapes=[pltpu.VMEM((SUBLANES, 2 * D_FF), F32)],
            compiler_params=params, name="ffn_seq",
        )(hn, h, *weights)
        return y, st[:, SUBLANES - (CONV_W - 1):]
    assert seq == 1
    st_w = (CONV_W - 1) * 2 * D_FF
    y, st = pl.pallas_call(
        functools.partial(_ffn_step_kernel, final=final),
        grid=(m // tm,), in_specs=in_specs + [row(st_w)],
        out_specs=(row(D_MODEL), row(st_w)),
        out_shape=(y_shape, jax.ShapeDtypeStruct((m, st_w), F32)),
        compiler_params=params, name="ffn_step",
    )(hn, h, *weights, prev.reshape(m, st_w))
    return y, st.reshape(m, CONV_W - 1, 2 * D_FF)


def _prep_weights(l, attn_norm, w_in, q_norm, kv_norm, w_uq, w_ukv, w_mla_o, w_sb_o, w_out, ffn_norm, w_up, conv_w,
                  conv_b, w_down, final_norm):
    w_in = w_in[l]
    o_kr = MLA_Q_LORA + MLA_KV_LORA
    o_sq = o_kr + MLA_ROPE_DIM
    o_g = o_sq + 3 * SB_WIDTH
    pad = jnp.zeros((D_MODEL, LANES - MLA_ROPE_DIM), w_in.dtype)
    w_in_p = jnp.concatenate([w_in[:, :o_sq], pad, w_in[:, o_sq:o_g]], axis=1).astype(BF16)
    w_gate = w_in[:, o_g:].astype(BF16)
    w_kv_t = w_in[:, o_sq + SB_WIDTH:o_g].T.astype(BF16)
    per_head = MLA_NOPE_DIM + MLA_ROPE_DIM
    uq3 = w_uq[l].reshape(MLA_Q_LORA, MLA_HEADS, per_head)
    uq_nope = uq3[:, :, :MLA_NOPE_DIM].reshape(MLA_Q_LORA, MLA_HEADS * MLA_NOPE_DIM)
    uq_rope = jnp.pad(uq3[:, :, MLA_NOPE_DIM:], ((0, 0), (0, 0), (0, LANES - MLA_ROPE_DIM)))
    w_uq2 = jnp.concatenate([uq_nope, uq_rope.reshape(MLA_Q_LORA, MLA_HEADS * LANES)], axis=1).astype(BF16)
    ukv3 = w_ukv[l].reshape(MLA_KV_LORA, MLA_HEADS, MLA_NOPE_DIM + MLA_V_DIM)
    uk_t = jnp.transpose(ukv3[:, :, :MLA_NOPE_DIM], (1, 2, 0))
    uv = jnp.transpose(ukv3[:, :, MLA_NOPE_DIM:], (1, 0, 2))
    zk = jnp.zeros_like(uk_t[0])
    zv = jnp.zeros_like(uv[0])
    w_uk_bd = jnp.stack([jnp.block([[uk_t[2 * p], zk], [zk, uk_t[2 * p + 1]]]) for p in range(MLA_HEADS // 2)])
    w_uv_bd = jnp.stack([jnp.block([[uv[2 * p], zv], [zv, uv[2 * p + 1]]]) for p in range(MLA_HEADS // 2)])
    return {
        "attn_norm": attn_norm[l][None, :], "w_in_p": w_in_p, "w_gate": w_gate, "w_kv_t": w_kv_t,
        "q_norm": q_norm[l][None, :], "kv_norm": kv_norm[l][None, :], "w_uq2": w_uq2,
        "w_uk_bd": w_uk_bd.astype(BF16), "w_uv_bd": w_uv_bd.astype(BF16),
        "w_mla_o": w_mla_o[l].astype(BF16), "w_sb_o": w_sb_o[l].astype(BF16), "w_out": w_out[l].astype(BF16),
        "ffn_norm": ffn_norm[l][None, :], "w_up": w_up[l].astype(BF16), "conv_w": conv_w[l],
        "conv_b": conv_b[l][None, :], "w_down": w_down[l].astype(BF16), "final_norm": final_norm[None, :],
    }


def _rope_tables(pos):
    inv = ROPE_BASE ** (-jnp.arange(ROPE_HALF, dtype=F32) / ROPE_HALF)
    ang = pos.astype(F32)[:, None] * inv[None, :]
    cos, sin = jnp.cos(ang), jnp.sin(ang)
    z = jnp.zeros_like(cos)
    zpad = jnp.zeros((pos.shape[0], LANES - MLA_ROPE_DIM), F32)
    return (jnp.concatenate([cos, cos, zpad], axis=1), jnp.concatenate([z, sin, zpad], axis=1),
            jnp.concatenate([-sin, z, zpad], axis=1))


def _tri(n):
    r = lax.broadcasted_iota(jnp.int32, (n, n), 0)
    c = lax.broadcasted_iota(jnp.int32, (n, n), 1)
    return (r > c).astype(BF16)


def _pick_tile(n, target):
    t = min(n, target)
    assert n % t == 0
    return t


def kernel(x_prompt, x_sample, cache_latent, cache_krope, cache_sb_k, cache_sb_v, state_conv, page_table, attn_norm, w_in, q_norm, kv_norm, w_uq, w_ukv, w_mla_o, w_sb_o, w_out, ffn_norm, w_up, conv_w, conv_b, w_down, final_norm):
    batch, seq, _ = x_prompt.shape
    db, dec_seq, _ = x_sample.shape
    assert dec_seq == 1
    depth = w_in.shape[0]
    n_pool = cache_latent.shape[1]
    n_pages = page_table.shape[1]
    mp, ms = batch * seq, db * dec_seq
    tabs_p = _rope_tables(jnp.arange(seq))
    tabs_s = _rope_tables(jnp.full((ms,), n_pages * PAGE_SIZE, jnp.int32))
    sb_t = _pick_tile(seq, 256)
    tri_p, tri_s = _tri(sb_t), _tri(PAGE_SIZE)
    tm_p = _pick_tile(seq, 512)
    tm_f = _pick_tile(seq, 512)
    tm_s = _pick_tile(ms, 128)

    def token_major(xt, n, t):
        xt = xt.reshape(-1, SB_HEADS, SB_HEAD_DIM, xt.shape[-1])
        return jnp.transpose(xt, (0, 3, 1, 2)).reshape(n, t, SB_HEADS, SB_HEAD_DIM)

    def keys_minor(pool):
        return jnp.transpose(pool, (0, 2, 3, 1)).reshape(n_pool, SB_WIDTH, PAGE_SIZE)

    hp = x_prompt.reshape(mp, D_MODEL)
    hs = x_sample.reshape(ms, D_MODEL)
    outs = [[] for _ in range(10)]
    for l in range(depth):
        final = l == depth - 1
        w = _prep_weights(l, attn_norm, w_in, q_norm, kv_norm, w_uq, w_ukv, w_mla_o, w_sb_o, w_out, ffn_norm, w_up,
                          conv_w, conv_b, w_down, final_norm)
        qcat, kcat, ckv, kr, sq, sk, sv, _, _ = _project(hs, tabs_s, 1, w, tm_s)
        olat, osb = _sample_attention(
            jnp.swapaxes(qcat, 0, 1), jnp.concatenate([ckv, kr], axis=1).reshape(ms, 1, QCAT),
            sq.reshape(ms, 1, SB_WIDTH), tri_s,
            cache_latent[l], jnp.swapaxes(cache_krope[l], 1, 2), keys_minor(cache_sb_k[l]), keys_minor(cache_sb_v[l]),
            page_table)
        h, hn = _mix(hs, olat.reshape(ms, MLA_HEADS * MLA_KV_LORA), osb.reshape(ms, SB_WIDTH), w, tm_s)
        hs, conv_st = _ffn(hn, h, state_conv[l], w, tm_s, 1, final)
        outs[5].append(ckv.reshape(db, dec_seq, MLA_KV_LORA))
        outs[6].append(kr[:, :MLA_ROPE_DIM].reshape(db, dec_seq, MLA_ROPE_DIM))
        outs[7].append(token_major(sk, db, dec_seq))
        outs[8].append(token_major(sv, db, dec_seq))
        outs[9].append(conv_st)
        qcat, kcat, ckv, kr, sq, sk, sv, skb, svb = _project(hp, tabs_p, seq // tm_p, w, tm_p)
        olat = _mla_prompt(qcat, kcat, batch, seq, tq=_pick_tile(seq, 256), tk=_pick_tile(seq, 512))
        osb = _sb_prompt(sq, skb, svb, tri_p, batch, seq, t=sb_t)
        h, hn = _mix(hp, olat, osb, w, tm_p)
        hp, conv_st = _ffn(hn, h, None, w, tm_f, seq, final)
        outs[0].append(ckv.reshape(batch, seq, MLA_KV_LORA))
        outs[1].append(kr[:, :MLA_ROPE_DIM].reshape(batch, seq, MLA_ROPE_DIM))
        outs[2].append(token_major(sk, batch, seq))
        outs[3].append(token_major(sv, batch, seq))
        outs[4].append(conv_st)
    return (hp.reshape(batch, seq, D_MODEL), hs.reshape(db, dec_seq, D_MODEL)) + tuple(jnp.stack(o) for o in outs)
```

```python
import functools

import jax
import jax.numpy as jnp
from jax import lax
from jax.experimental import pallas as pl
from jax.experimental.pallas import tpu as pltpu

D_MODEL = 1024
PAGE_SIZE = 128
MLA_HEADS = 8
MLA_NOPE_DIM = 64
MLA_ROPE_DIM = 32
MLA_V_DIM = 64
MLA_Q_LORA = 256
MLA_KV_LORA = 128
ROPE_BASE = 10000.0
MLA_SCALE = (MLA_NOPE_DIM + MLA_ROPE_DIM) ** -0.5
SB_HEADS = 8
SB_HEAD_DIM = 64
SB_WIDTH = SB_HEADS * SB_HEAD_DIM
SB_SCALE = SB_HEAD_DIM ** -0.5
D_FF = 2816
CONV_W = 3
NORM_EPS = 1e-6

LANES = 128
QCAT = 2 * LANES
ROPE_HALF = MLA_ROPE_DIM // 2
MASK_VALUE = -1e30
VMEM_LIMIT = 56 * 1024 * 1024

BF16 = jnp.bfloat16
F32 = jnp.float32

_NT = (((1,), (1,)), ((), ()))


def _dot(a, b):
    return jnp.dot(a, b, preferred_element_type=F32)


def _dot_nt(a, b):
    return lax.dot_general(a, b, _NT, preferred_element_type=F32)


def _rmsnorm(x, g):
    return x * lax.rsqrt(jnp.mean(x * x, axis=-1, keepdims=True) + NORM_EPS) * g


def _sigmoid(x):
    return 1.0 / (1.0 + jnp.exp(-x))


def _rope(x, c, s1, s2):
    return x * c + pltpu.roll(x, ROPE_HALF, 1) * s1 + pltpu.roll(x, LANES - ROPE_HALF, 1) * s2


LOG2E = 1.4426950408889634


def _log2_sigmoid_pair(z2):
    soft = jnp.log2(1.0 + jnp.exp2(-jnp.abs(z2)))
    ls = jnp.minimum(z2, 0.0) - soft
    return ls, ls - z2


def _split_bf16(x):
    hi = x.astype(BF16)
    lo = (x - hi.astype(F32)).astype(BF16)
    return hi, lo


def _proj_kernel(x_ref, g_ref, w_ref, wkvt_ref, qn_ref, kvn_ref, wuq_ref, wuk_ref, c_ref, s1_ref, s2_ref,
                 qcat_ref, kcat_ref, ckv_ref, kr_ref, sq_ref, skt_ref, svt_ref, skb_ref, svb_ref):
    xn = _rmsnorm(x_ref[...], g_ref[...]).astype(BF16)
    y = _dot(xn, w_ref[...])
    kvt = _dot_nt(wkvt_ref[...], xn)
    skt_ref[...] = kvt[0:SB_WIDTH]
    svt_ref[...] = kvt[SB_WIDTH:2 * SB_WIDTH]
    c, s1, s2 = c_ref[...], s1_ref[...], s2_ref[...]

    cqn = _rmsnorm(y[:, 0:MLA_Q_LORA], qn_ref[...]).astype(BF16)
    q2 = _dot(cqn, wuq_ref[...])
    nope_w = MLA_HEADS * MLA_NOPE_DIM
    for p in range(MLA_HEADS // 2):
        ql = _dot(q2[:, p * LANES:(p + 1) * LANES].astype(BF16), wuk_ref[p]) * (MLA_SCALE * LOG2E)
        for e in range(2):
            qcat_ref[2 * p + e, :, 0:LANES] = ql[:, e * LANES:(e + 1) * LANES].astype(BF16)
    for h in range(MLA_HEADS):
        xr = q2[:, nope_w + h * LANES:nope_w + (h + 1) * LANES]
        qcat_ref[h, :, LANES:QCAT] = (_rope(xr, c, s1, s2) * (MLA_SCALE * LOG2E)).astype(BF16)

    o = MLA_Q_LORA
    ckv = _rmsnorm(y[:, o:o + MLA_KV_LORA], kvn_ref[...])
    ckv_ref[...] = ckv
    kcat_ref[:, 0:LANES] = ckv.astype(BF16)
    o += MLA_KV_LORA
    kr = _rope(y[:, o:o + LANES], c, s1, s2)
    kr_ref[...] = kr
    kcat_ref[:, LANES:QCAT] = kr.astype(BF16)
    o += LANES
    sq_ref[...] = (y[:, o:o + SB_WIDTH] * (SB_SCALE * LOG2E)).astype(BF16)
    o += SB_WIDTH
    skb_ref[...] = y[:, o:o + SB_WIDTH].astype(BF16)
    o += SB_WIDTH
    svb_ref[...] = y[:, o:o + SB_WIDTH].astype(BF16)


def _project(x, tabs, tiles_per_seq, w, tm):
    m = x.shape[0]
    n_seq = m // (tiles_per_seq * tm)
    full = lambda a: pl.BlockSpec(a.shape, lambda i: (0,) * a.ndim)
    row = lambda n: pl.BlockSpec((tm, n), lambda i: (i, 0))
    tab = pl.BlockSpec((tm, LANES), lambda i: (i % tiles_per_seq, 0))
    feat_major = pl.BlockSpec((None, SB_WIDTH, tm), lambda i: (i // tiles_per_seq, 0, i % tiles_per_seq))
    feat_major_shape = jax.ShapeDtypeStruct((n_seq, SB_WIDTH, tiles_per_seq * tm), F32)
    out_shape = (
        jax.ShapeDtypeStruct((MLA_HEADS, m, QCAT), BF16),
        jax.ShapeDtypeStruct((m, QCAT), BF16),
        jax.ShapeDtypeStruct((m, MLA_KV_LORA), F32),
        jax.ShapeDtypeStruct((m, LANES), F32),
        jax.ShapeDtypeStruct((m, SB_WIDTH), BF16),
        feat_major_shape,
        feat_major_shape,
        jax.ShapeDtypeStruct((m, SB_WIDTH), BF16),
        jax.ShapeDtypeStruct((m, SB_WIDTH), BF16),
    )
    out_specs = (
        pl.BlockSpec((MLA_HEADS, tm, QCAT), lambda i: (0, i, 0)),
        row(QCAT), row(MLA_KV_LORA), row(LANES), row(SB_WIDTH), feat_major, feat_major,
        row(SB_WIDTH), row(SB_WIDTH),
    )
    ins = (x, w["attn_norm"], w["w_in_p"], w["w_kv_t"], w["q_norm"], w["kv_norm"], w["w_uq2"], w["w_uk_bd"]) + tabs
    in_specs = [row(D_MODEL)] + [full(a) for a in ins[1:8]] + [tab, tab, tab]
    return pl.pallas_call(
        _proj_kernel, grid=(m // tm,), in_specs=in_specs, out_specs=out_specs, out_shape=out_shape,
        compiler_params=pltpu.CompilerParams(dimension_semantics=("arbitrary",), vmem_limit_bytes=VMEM_LIMIT),
        name="proj",
    )(*ins)


MLA_ROW_GROUPS = 2


def _mla_kernel(q_ref, k_ref, o_ref, m_s, l_s, acc_s, *, tq, tk):
    i = pl.program_id(1)
    rows = MLA_HEADS * tq
    q = q_ref[...].reshape(rows, QCAT)
    m_s[...] = jnp.full((rows, LANES), MASK_VALUE, F32)
    l_s[...] = jnp.zeros((rows, LANES), F32)
    acc_s[...] = jnp.zeros((rows, LANES), F32)

    gr = rows // MLA_ROW_GROUPS
    groups = [slice(n * gr, (n + 1) * gr) for n in range(MLA_ROW_GROUPS)]

    def block(j, masked):
        k = k_ref[pl.ds(pl.multiple_of(j * tk, tk), tk), :]
        s = [_dot_nt(q[g], k) for g in groups]
        if masked:
            qpos = i * tq + jnp.bitwise_and(lax.broadcasted_iota(jnp.int32, (gr, tk), 0), tq - 1)
            kpos = j * tk + lax.broadcasted_iota(jnp.int32, (gr, tk), 1)
            s = [jnp.where(kpos <= qpos, x, MASK_VALUE) for x in s]
        m_prev = [m_s[g] for g in groups]
        m_next = [jnp.maximum(mp, jnp.max(x, axis=1, keepdims=True)) for mp, x in zip(m_prev, s)]
        alpha = [jnp.exp2(mp - mn) for mp, mn in zip(m_prev, m_next)]
        p = [jnp.concatenate([jnp.exp2(x[:, c * LANES:(c + 1) * LANES] - mn) for c in range(tk // LANES)], axis=1)
             for x, mn in zip(s, m_next)]
        pv = [_dot(x.astype(BF16), k[:, 0:MLA_KV_LORA]) for x in p]
        for n, g in enumerate(groups):
            l_s[g] = alpha[n] * l_s[g] + jnp.sum(p[n], axis=1, keepdims=True)
            acc_s[g] = alpha[n] * acc_s[g] + pv[n]
            m_s[g] = m_next[n]

    n_full = (i * tq) // tk

    def body(j, carry):
        block(j, False)
        return carry

    lax.fori_loop(0, n_full, body, 0)
    block(n_full, True)
    o = acc_s[...] / l_s[...]
    for h in range(MLA_HEADS):
        o_ref[:, h * LANES:(h + 1) * LANES] = o[h * tq:(h + 1) * tq].astype(o_ref.dtype)


def _mla_prompt(qcat, kcat, batch, seq, tq=128, tk=256):
    assert tk % tq == 0 and seq % tk == 0 and tq & (tq - 1) == 0
    m = batch * seq
    nq = seq // tq
    rows = MLA_HEADS * tq
    return pl.pallas_call(
        functools.partial(_mla_kernel, tq=tq, tk=tk),
        grid=(batch, nq),
        in_specs=[pl.BlockSpec((MLA_HEADS, tq, QCAT), lambda b, i: (0, b * nq + i, 0)),
                  pl.BlockSpec((seq, QCAT), lambda b, i: (b, 0))],
        out_specs=pl.BlockSpec((tq, MLA_HEADS * MLA_KV_LORA), lambda b, i: (b * nq + i, 0)),
        out_shape=jax.ShapeDtypeStruct((m, MLA_HEADS * MLA_KV_LORA), BF16),
        scratch_shapes=[pltpu.VMEM((rows, LANES), F32)] * 3,
        compiler_params=pltpu.CompilerParams(dimension_semantics=("arbitrary", "arbitrary"),
                                             vmem_limit_bytes=VMEM_LIMIT),
        name="mla_prompt",
    )(qcat, kcat)


def _sb_kernel(q_ref, k_ref, v_ref, t_ref, o_ref, q2_s, acc_s, carry_s, *, t, pairs):
    i = pl.program_id(2)
    tri = t_ref[...]
    lane = lax.broadcasted_iota(jnp.int32, (t, LANES), 1)
    first = lane < SB_HEAD_DIM
    row = jnp.bitwise_and(lax.broadcasted_iota(jnp.int32, (2 * t, t), 0), t - 1)
    strict = lax.broadcasted_iota(jnp.int32, (2 * t, t), 1) < row
    for p in range(pairs):
        q = q_ref[:, p * LANES:(p + 1) * LANES].astype(F32)
        q2_s[p, 0:t, :] = jnp.where(first, q, 0.0).astype(BF16)
        q2_s[p, t:2 * t, :] = jnp.where(first, 0.0, q).astype(BF16)
    acc_s[...] = jnp.zeros(acc_s.shape, F32)
    carry_s[...] = jnp.zeros(carry_s.shape, F32)

    def block(j, masked):
        start = pl.multiple_of(j * t, t)
        ps = range(pairs)
        z = [_dot_nt(q2_s[p], k_ref[pl.ds(start, t), p * LANES:(p + 1) * LANES]) for p in ps]
        ls, lk = zip(*[_log2_sigmoid_pair(z[p]) for p in ps])
        if masked:
            lk = [jnp.where(strict, lk[p], 0.0) for p in ps]
        split = [_split_bf16(lk[p]) for p in ps]
        after = [_dot(split[p][0], tri) + _dot(split[p][1], tri) for p in ps]
        carry = [carry_s[p] for p in ps]
        log_a = [jnp.concatenate(
            [ls[p][:, c * LANES:(c + 1) * LANES] + after[p][:, c * LANES:(c + 1) * LANES] + carry[p]
             for c in range(t // LANES)], axis=1) for p in ps]
        if masked:
            log_a = [jnp.where(strict, log_a[p], MASK_VALUE) for p in ps]
        a = [jnp.exp2(log_a[p]).astype(BF16) for p in ps]
        pv = [_dot(a[p], v_ref[pl.ds(start, t), p * LANES:(p + 1) * LANES]) for p in ps]
        for p in ps:
            acc_s[p] += pv[p]
            carry_s[p] = carry[p] + jnp.sum(lk[p], axis=1, keepdims=True)

    block(i, True)

    def body(n, c):
        block(i - 1 - n, False)
        return c

    lax.fori_loop(0, i, body, 0)
    for p in range(pairs):
        o_ref[:, p * LANES:(p + 1) * LANES] = jnp.where(first, acc_s[p, 0:t, :], acc_s[p, t:2 * t, :]).astype(o_ref.dtype)


def _sb_prompt(sq, skb, svb, tri, batch, seq, t=256, pairs=4):
    assert seq % t == 0 and t & (t - 1) == 0 and (SB_HEADS // 2) % pairs == 0
    m = batch * seq
    nq = seq // t
    w = pairs * LANES
    return pl.pallas_call(
        functools.partial(_sb_kernel, t=t, pairs=pairs),
        grid=(batch, SB_HEADS // 2 // pairs, nq),
        in_specs=[pl.BlockSpec((t, w), lambda b, p, i: (b * nq + i, p)),
                  pl.BlockSpec((seq, w), lambda b, p, i: (b, p)),
                  pl.BlockSpec((seq, w), lambda b, p, i: (b, p)),
                  pl.BlockSpec((t, t), lambda b, p, i: (0, 0))],
        out_specs=pl.BlockSpec((t, w), lambda b, p, i: (b * nq + i, p)),
        out_shape=jax.ShapeDtypeStruct((m, SB_WIDTH), BF16),
        scratch_shapes=[pltpu.VMEM((pairs, 2 * t, LANES), BF16), pltpu.VMEM((pairs, 2 * t, LANES), F32),
                        pltpu.VMEM((pairs, 2 * t, LANES), F32)],
        compiler_params=pltpu.CompilerParams(dimension_semantics=("arbitrary",) * 3,
                                             vmem_limit_bytes=VMEM_LIMIT),
        name="sb_prompt",
    )(sq, skb, svb, tri)


def _sample_attn_kernel(pt_ref, qcat_ref, knew_ref, sq_ref, t_ref, lat_hbm, kr_hbm, sbk_hbm, sbv_hbm,
                        olat_ref, osb_ref, lat_buf, kr_buf, sbk_buf, sbv_buf, sems, m_s, l_s, acc_s, carry_s, sbacc_s,
                        *, group, n_pages):
    b, g = pl.program_id(0), pl.program_id(1)
    steps = pl.num_programs(1)
    n = b * steps + g
    slot = lax.rem(n, 2)
    pools = ((lat_hbm, lat_buf), (kr_hbm, kr_buf), (sbk_hbm, sbk_buf), (sbv_hbm, sbv_buf))

    def page_copy(idx, page, slot_, r):
        hbm, buf = pools[idx]
        return pltpu.make_async_copy(hbm.at[page], buf.at[slot_, r], sems.at[slot_, idx])

    def start_step(b_, g_, slot_):
        for r in range(group):
            page = pt_ref[b_, n_pages - 1 - (g_ * group + r)]
            for idx in range(len(pools)):
                page_copy(idx, page, slot_, r).start(priority=(r + idx) % 2)

    @pl.when(n == 0)
    def _():
        start_step(0, 0, 0)

    @pl.when(n + 1 < pl.num_programs(0) * steps)
    def _():
        wrap = g + 1 == steps
        start_step(jnp.where(wrap, b + 1, b), jnp.where(wrap, 0, g + 1), 1 - slot)

    for r in range(group):
        for idx in range(len(pools)):
            page_copy(idx, 0, slot, r).wait()

    lat_refs = [lat_buf.at[slot, r] for r in range(group)]
    kr_refs = [kr_buf.at[slot, r] for r in range(group)]
    sbk_refs = [sbk_buf.at[slot, r] for r in range(group)]
    sbv_refs = [sbv_buf.at[slot, r] for r in range(group)]
    qc = qcat_ref[...]

    @pl.when(g == 0)
    def _():
        kn = knew_ref[...].astype(F32)
        s_self = jnp.sum(qc.astype(F32) * kn, axis=1, keepdims=True)
        m_s[...] = jnp.broadcast_to(s_self, (MLA_HEADS, LANES))
        l_s[...] = jnp.ones((MLA_HEADS, LANES), F32)
        acc_s[...] = jnp.broadcast_to(kn[:, 0:MLA_KV_LORA], (MLA_HEADS, MLA_KV_LORA))
        carry_s[...] = jnp.zeros((SB_HEADS, LANES), F32)
        sbacc_s[...] = jnp.zeros((SB_HEADS, SB_WIDTH), F32)

    q_lat = qc[:, 0:LANES]
    q_rope = qc[:, LANES:LANES + MLA_ROPE_DIM]
    head_of_lane = lax.broadcasted_iota(jnp.int32, (SB_HEADS, SB_WIDTH), 1) // SB_HEAD_DIM
    own_head = head_of_lane == lax.broadcasted_iota(jnp.int32, (SB_HEADS, SB_WIDTH), 0)
    sq = jnp.broadcast_to(sq_ref[...].astype(F32), (SB_HEADS, SB_WIDTH))
    q_sb = jnp.where(own_head, sq, 0.0).astype(BF16)
    tri = t_ref[...]

    pages = range(group)
    lat = [lat_refs[r][...].astype(BF16) for r in pages]
    s = [_dot_nt(q_lat, lat[r]) + _dot(q_rope, kr_refs[r][...].astype(BF16)) for r in pages]
    m = m_s[...]
    m_next = jnp.maximum(m, jnp.max(functools.reduce(jnp.maximum, s), axis=1, keepdims=True))
    alpha = jnp.exp2(m - m_next)
    p = [jnp.exp2(s[r] - m_next) for r in pages]
    l = alpha * l_s[...] + jnp.sum(functools.reduce(jnp.add, p), axis=1, keepdims=True)
    acc = alpha * acc_s[...] + functools.reduce(jnp.add, [_dot(p[r].astype(BF16), lat[r]) for r in pages])
    m_s[...], l_s[...], acc_s[...] = m_next, l, acc

    z = jnp.concatenate([_dot(q_sb, sbk_refs[r][...].astype(BF16)) for r in pages], axis=0)
    ls, lk = _log2_sigmoid_pair(z)
    hi, lo = _split_bf16(lk)
    after = _dot(hi, tri) + _dot(lo, tri)
    page_sum = jnp.sum(lk, axis=1, keepdims=True)
    log_a = ls + after
    carry = carry_s[...]
    pv = []
    for r in pages:
        rows = slice(r * SB_HEADS, (r + 1) * SB_HEADS)
        a = jnp.exp2(log_a[rows] + carry)
        pv.append(_dot_nt(a.astype(BF16), sbv_refs[r][...].astype(BF16)))
        carry = carry + page_sum[rows]
    sbacc = sbacc_s[...] + functools.reduce(jnp.add, pv)
    carry_s[...], sbacc_s[...] = carry, sbacc

    @pl.when(g == pl.num_programs(1) - 1)
    def _():
        olat_ref[...] = acc / l
        osb_ref[...] = jnp.sum(jnp.where(own_head, sbacc, 0.0), axis=0, keepdims=True)


def _sample_attention(qcat_s, knew, sq_s, tri, lat_pool, kr_pool, sbk_pool, sbv_pool, page_table, group=16):
    db, n_pages = page_table.shape
    assert n_pages % group == 0
    steps = n_pages // group

    per_b = lambda shape: pl.BlockSpec((None,) + shape, lambda b, g, pt: (b, 0, 0))
    pools = (lat_pool, kr_pool, sbk_pool, sbv_pool)
    in_specs = [per_b((MLA_HEADS, QCAT)), per_b((1, QCAT)), per_b((1, SB_WIDTH)),
                pl.BlockSpec((PAGE_SIZE, PAGE_SIZE), lambda b, g, pt: (0, 0))]
    in_specs += [pl.BlockSpec(memory_space=pl.ANY)] * len(pools)
    ring = [pltpu.VMEM((2, group) + pool.shape[1:], pool.dtype) for pool in pools]
    grid_spec = pltpu.PrefetchScalarGridSpec(
        num_scalar_prefetch=1, grid=(db, steps), in_specs=in_specs,
        out_specs=(per_b((MLA_HEADS, MLA_KV_LORA)), per_b((1, SB_WIDTH))),
        scratch_shapes=ring + [pltpu.SemaphoreType.DMA((2, len(pools)))]
        + [pltpu.VMEM((MLA_HEADS, LANES), F32)] * 4 + [pltpu.VMEM((SB_HEADS, SB_WIDTH), F32)])
    operands = [qcat_s, knew, sq_s, tri, *pools]
    return pl.pallas_call(
        functools.partial(_sample_attn_kernel, group=group, n_pages=n_pages),
        grid_spec=grid_spec,
        out_shape=(jax.ShapeDtypeStruct((db, MLA_HEADS, MLA_KV_LORA), F32),
                   jax.ShapeDtypeStruct((db, 1, SB_WIDTH), F32)),
        compiler_params=pltpu.CompilerParams(dimension_semantics=("arbitrary", "arbitrary"),
                                             vmem_limit_bytes=VMEM_LIMIT),
        name="sample_attn",
    )(page_table, *operands)


def _mix_kernel(x_ref, g_ref, wg_ref, olat_ref, wuv_ref, wmo_ref, osb_ref, wso_ref, wout_ref, g2_ref,
                h_ref, hn_ref):
    x = x_ref[...]
    xn = _rmsnorm(x, g_ref[...]).astype(BF16)
    gates = _dot(xn, wg_ref[...])
    olat = olat_ref[...].astype(BF16)
    om = jnp.concatenate([_dot(olat[:, p * QCAT:(p + 1) * QCAT], wuv_ref[p]) for p in range(MLA_HEADS // 2)],
                         axis=1).astype(BF16)
    o_mla = _dot(om, wmo_ref[...])
    o_sb = _dot(osb_ref[...].astype(BF16), wso_ref[...])
    mix = _sigmoid(gates[:, 0:D_MODEL]) * o_mla + _sigmoid(gates[:, D_MODEL:2 * D_MODEL]) * o_sb
    h = x + _dot(mix.astype(BF16), wout_ref[...])
    h_ref[...] = h
    hn_ref[...] = _rmsnorm(h, g2_ref[...]).astype(BF16)


def _mix(x, olat, osb, w, tm):
    m = x.shape[0]
    full = lambda a: pl.BlockSpec(a.shape, lambda i: (0,) * a.ndim)
    row = lambda n: pl.BlockSpec((tm, n), lambda i: (i, 0))
    ins = (x, w["attn_norm"], w["w_gate"], olat, w["w_uv_bd"], w["w_mla_o"], osb, w["w_sb_o"], w["w_out"],
           w["ffn_norm"])
    in_specs = [row(D_MODEL), full(ins[1]), full(ins[2]), row(olat.shape[1]), full(ins[4]), full(ins[5]),
                row(SB_WIDTH), full(ins[7]), full(ins[8]), full(ins[9])]
    return pl.pallas_call(
        _mix_kernel, grid=(m // tm,), in_specs=in_specs,
        out_specs=(row(D_MODEL), row(D_MODEL)),
        out_shape=(jax.ShapeDtypeStruct((m, D_MODEL), F32), jax.ShapeDtypeStruct((m, D_MODEL), BF16)),
        compiler_params=pltpu.CompilerParams(dimension_semantics=("arbitrary",), vmem_limit_bytes=VMEM_LIMIT),
        name="mix",
    )(*ins)


FFN_CHUNK = 256


SUBLANES = 8


def _ffn_chunks(hn, wup_ref, cw_ref, cb_ref, wdn_ref, shifted):
    n_chunks = D_FF // FFN_CHUNK

    def up(c):
        cols = [slice(half * D_FF + c * FFN_CHUNK, half * D_FF + (c + 1) * FFN_CHUNK) for half in range(2)]
        return [(_dot(hn, wup_ref[:, s]), s) for s in cols]

    acc = jnp.zeros((hn.shape[0], D_MODEL), F32)
    ahead = 2
    queue = [up(c) for c in range(ahead)]
    for c in range(n_chunks):
        cur = queue.pop(0)
        if c + ahead < n_chunks:
            queue.append(up(c + ahead))
        conv = []
        for u, cols in cur:
            u1, u2 = shifted(u, cols)
            conv.append(cb_ref[:, cols] + cw_ref[0:1, cols] * u2 + cw_ref[1:2, cols] * u1 + cw_ref[2:3, cols] * u)
        a, b = conv
        gated = (a * _sigmoid(a) * b).astype(BF16)
        acc = acc + _dot(gated, wdn_ref[c * FFN_CHUNK:(c + 1) * FFN_CHUNK, :])
    return acc


def _ffn_out(h_ref, acc, gf_ref, y_ref, final):
    y = h_ref[...] + acc
    y_ref[...] = _rmsnorm(y, gf_ref[...]) if final else y


def _ffn_seq_kernel(hn_ref, h_ref, wup_ref, cw_ref, cb_ref, wdn_ref, gf_ref, y_ref, st_ref, tail_s, *,
                    tm, tiles_per_seq, final):
    @pl.when(pl.program_id(0) % tiles_per_seq == 0)
    def _():
        tail_s[...] = jnp.zeros(tail_s.shape, F32)

    rowid = lax.broadcasted_iota(jnp.int32, (tm, FFN_CHUNK), 0)

    def shifted(u, cols):
        p0, p1 = tail_s[SUBLANES - 2:SUBLANES - 1, cols], tail_s[SUBLANES - 1:SUBLANES, cols]
        u1 = jnp.where(rowid == 0, p1, pltpu.roll(u, 1, 0))
        u2 = jnp.where(rowid == 0, p0, jnp.where(rowid == 1, p1, pltpu.roll(u, 2, 0)))
        tail = u[tm - SUBLANES:tm]
        tail_s[:, cols] = tail
        st_ref[0, :, cols] = tail
        return u1, u2

    acc = _ffn_chunks(hn_ref[...], wup_ref, cw_ref, cb_ref, wdn_ref, shifted)
    _ffn_out(h_ref, acc, gf_ref, y_ref, final)


def _ffn_step_kernel(hn_ref, h_ref, wup_ref, cw_ref, cb_ref, wdn_ref, gf_ref, prev_ref, y_ref, st_ref, *, final):
    def shifted(u, cols):
        later = slice(2 * D_FF + cols.start, 2 * D_FF + cols.stop)
        p0, p1 = prev_ref[:, cols], prev_ref[:, later]
        st_ref[:, cols] = p1
        st_ref[:, later] = u
        return p1, p0

    acc = _ffn_chunks(hn_ref[...], wup_ref, cw_ref, cb_ref, wdn_ref, shifted)
    _ffn_out(h_ref, acc, gf_ref, y_ref, final)


def _ffn(hn, h, prev, w, tm, seq, final):
    m = hn.shape[0]
    full = lambda a: pl.BlockSpec(a.shape, lambda i: (0,) * a.ndim)
    row = lambda n: pl.BlockSpec((tm, n), lambda i: (i, 0))
    weights = (w["w_up"], w["conv_w"], w["conv_b"], w["w_down"], w["final_norm"])
    in_specs = [row(D_MODEL), row(D_MODEL)] + [full(a) for a in weights]
    params = pltpu.CompilerParams(dimension_semantics=("arbitrary",), vmem_limit_bytes=VMEM_LIMIT)
    y_shape = jax.ShapeDtypeStruct((m, D_MODEL), F32)
    if prev is None:
        assert seq % tm == 0 and tm >= SUBLANES
        tiles_per_seq = seq // tm
        y, st = pl.pallas_call(
            functools.partial(_ffn_seq_kernel, tm=tm, tiles_per_seq=tiles_per_seq, final=final),
            grid=(m // tm,), in_specs=in_specs,
            out_specs=(row(D_MODEL), pl.BlockSpec((1, SUBLANES, 2 * D_FF), lambda i: (i // tiles_per_seq, 0, 0))),
            out_shape=(y_shape, jax.ShapeDtypeStruct((m // seq, SUBLANES, 2 * D_FF), F32)),
            scratch_shapes=[pltpu.VMEM((SUBLANES, 2 * D_FF), F32)],
            compiler_params=params, name="ffn_seq",
        )(hn, h, *weights)
        return y, st[:, SUBLANES - (CONV_W - 1):]
    assert seq == 1
    st_w = (CONV_W - 1) * 2 * D_FF
    y, st = pl.pallas_call(
        functools.partial(_ffn_step_kernel, final=final),
        grid=(m // tm,), in_specs=in_specs + [row(st_w)],
        out_specs=(row(D_MODEL), row(st_w)),
        out_shape=(y_shape, jax.ShapeDtypeStruct((m, st_w), F32)),
        compiler_params=params, name="ffn_step",
    )(hn, h, *weights, prev.reshape(m, st_w))
    return y, st.reshape(m, CONV_W - 1, 2 * D_FF)


def _prep_weights(l, attn_norm, w_in, q_norm, kv_norm, w_uq, w_ukv, w_mla_o, w_sb_o, w_out, ffn_norm, w_up, conv_w,
                  conv_b, w_down, final_norm):
    w_in = w_in[l]
    o_kr = MLA_Q_LORA + MLA_KV_LORA
    o_sq = o_kr + MLA_ROPE_DIM
    o_g = o_sq + 3 * SB_WIDTH
    pad = jnp.zeros((D_MODEL, LANES - MLA_ROPE_DIM), w_in.dtype)
    w_in_p = jnp.concatenate([w_in[:, :o_sq], pad, w_in[:, o_sq:o_g]], axis=1).astype(BF16)
    w_gate = w_in[:, o_g:].astype(BF16)
    w_kv_t = w_in[:, o_sq + SB_WIDTH:o_g].T.astype(BF16)
    per_head = MLA_NOPE_DIM + MLA_ROPE_DIM
    uq3 = w_uq[l].reshape(MLA_Q_LORA, MLA_HEADS, per_head)
    uq_nope = uq3[:, :, :MLA_NOPE_DIM].reshape(MLA_Q_LORA, MLA_HEADS * MLA_NOPE_DIM)
    uq_rope = jnp.pad(uq3[:, :, MLA_NOPE_DIM:], ((0, 0), (0, 0), (0, LANES - MLA_ROPE_DIM)))
    w_uq2 = jnp.concatenate([uq_nope, uq_rope.reshape(MLA_Q_LORA, MLA_HEADS * LANES)], axis=1).astype(BF16)
    ukv3 = w_ukv[l].reshape(MLA_KV_LORA, MLA_HEADS, MLA_NOPE_DIM + MLA_V_DIM)
    uk_t = jnp.transpose(ukv3[:, :, :MLA_NOPE_DIM], (1, 2, 0))
    uv = jnp.transpose(ukv3[:, :, MLA_NOPE_DIM:], (1, 0, 2))
    zk = jnp.zeros_like(uk_t[0])
    zv = jnp.zeros_like(uv[0])
    w_uk_bd = jnp.stack([jnp.block([[uk_t[2 * p], zk], [zk, uk_t[2 * p + 1]]]) for p in range(MLA_HEADS // 2)])
    w_uv_bd = jnp.stack([jnp.block([[uv[2 * p], zv], [zv, uv[2 * p + 1]]]) for p in range(MLA_HEADS // 2)])
    return {
        "attn_norm": attn_norm[l][None, :], "w_in_p": w_in_p, "w_gate": w_gate, "w_kv_t": w_kv_t,
        "q_norm": q_norm[l][None, :], "kv_norm": kv_norm[l][None, :], "w_uq2": w_uq2,
        "w_uk_bd": w_uk_bd.astype(BF16), "w_uv_bd": w_uv_bd.astype(BF16),
        "w_mla_o": w_mla_o[l].astype(BF16), "w_sb_o": w_sb_o[l].astype(BF16), "w_out": w_out[l].astype(BF16),
        "ffn_norm": ffn_norm[l][None, :], "w_up": w_up[l].astype(BF16), "conv_w": conv_w[l],
        "conv_b": conv_b[l][None, :], "w_down": w_down[l].astype(BF16), "final_norm": final_norm[None, :],
    }


def _rope_tables(pos):
    inv = ROPE_BASE ** (-jnp.arange(ROPE_HALF, dtype=F32) / ROPE_HALF)
    ang = pos.astype(F32)[:, None] * inv[None, :]
    cos, sin = jnp.cos(ang), jnp.sin(ang)
    z = jnp.zeros_like(cos)
    zpad = jnp.zeros((pos.shape[0], LANES - MLA_ROPE_DIM), F32)
    return (jnp.concatenate([cos, cos, zpad], axis=1), jnp.concatenate([z, sin, zpad], axis=1),
            jnp.concatenate([-sin, z, zpad], axis=1))


def _tri(n):
    r = lax.broadcasted_iota(jnp.int32, (n, n), 0)
    c = lax.broadcasted_iota(jnp.int32, (n, n), 1)
    return (r > c).astype(BF16)


def _pick_tile(n, target):
    t = min(n, target)
    assert n % t == 0
    return t


def kernel(x_prompt, x_sample, cache_latent, cache_krope, cache_sb_k, cache_sb_v, state_conv, page_table, attn_norm, w_in, q_norm, kv_norm, w_uq, w_ukv, w_mla_o, w_sb_o, w_out, ffn_norm, w_up, conv_w, conv_b, w_down, final_norm):
    batch, seq, _ = x_prompt.shape
    db, dec_seq, _ = x_sample.shape
    assert dec_seq == 1
    depth = w_in.shape[0]
    n_pool = cache_latent.shape[1]
    n_pages = page_table.shape[1]
    mp, ms = batch * seq, db * dec_seq
    tabs_p = _rope_tables(jnp.arange(seq))
    tabs_s = _rope_tables(jnp.full((ms,), n_pages * PAGE_SIZE, jnp.int32))
    sb_t = _pick_tile(seq, 256)
    tri_p, tri_s = _tri(sb_t), _tri(PAGE_SIZE)
    tm_p = _pick_tile(seq, 512)
    tm_f = _pick_tile(seq, 512)
    tm_s = _pick_tile(ms, 128)

    def token_major(xt, n, t):
        xt = xt.reshape(-1, SB_HEADS, SB_HEAD_DIM, xt.shape[-1])
        return jnp.transpose(xt, (0, 3, 1, 2)).reshape(n, t, SB_HEADS, SB_HEAD_DIM)

    def keys_minor(pool):
        return jnp.transpose(pool, (0, 2, 3, 1)).reshape(n_pool, SB_WIDTH, PAGE_SIZE)

    hp = x_prompt.reshape(mp, D_MODEL)
    hs = x_sample.reshape(ms, D_MODEL)
    outs = [[] for _ in range(10)]
    for l in range(depth):
        final = l == depth - 1
        w = _prep_weights(l, attn_norm, w_in, q_norm, kv_norm, w_uq, w_ukv, w_mla_o, w_sb_o, w_out, ffn_norm, w_up,
                          conv_w, conv_b, w_down, final_norm)
        qcat, kcat, ckv, kr, sq, sk, sv, _, _ = _project(hs, tabs_s, 1, w, tm_s)
        olat, osb = _sample_attention(
            jnp.swapaxes(qcat, 0, 1), jnp.concatenate([ckv, kr], axis=1).reshape(ms, 1, QCAT),
            sq.reshape(ms, 1, SB_WIDTH), tri_s,
            cache_latent[l], jnp.swapaxes(cache_krope[l], 1, 2), keys_minor(cache_sb_k[l]), keys_minor(cache_sb_v[l]),
            page_table)
        h, hn = _mix(hs, olat.reshape(ms, MLA_HEADS * MLA_KV_LORA), osb.reshape(ms, SB_WIDTH), w, tm_s)
        hs, conv_st = _ffn(hn, h, state_conv[l], w, tm_s, 1, final)
        outs[5].append(ckv.reshape(db, dec_seq, MLA_KV_LORA))
        outs[6].append(kr[:, :MLA_ROPE_DIM].reshape(db, dec_seq, MLA_ROPE_DIM))
        outs[7].append(token_major(sk, db, dec_seq))
        outs[8].append(token_major(sv, db, dec_seq))
        outs[9].append(conv_st)
        qcat, kcat, ckv, kr, sq, sk, sv, skb, svb = _project(hp, tabs_p, seq // tm_p, w, tm_p)
        olat = _mla_prompt(qcat, kcat, batch, seq, tq=_pick_tile(seq, 256), tk=_pick_tile(seq, 512))
        osb = _sb_prompt(sq, skb, svb, tri_p, batch, seq, t=sb_t)
        h, hn = _mix(hp, olat, osb, w, tm_p)
        hp, conv_st = _ffn(hn, h, None, w, tm_f, seq, final)
        outs[0].append(ckv.reshape(batch, seq, MLA_KV_LORA))
        outs[1].append(kr[:, :MLA_ROPE_DIM].reshape(batch, seq, MLA_ROPE_DIM))
        outs[2].append(token_major(sk, batch, seq))
        outs[3].append(token_major(sv, batch, seq))
        outs[4].append(conv_st)
    return (hp.reshape(batch, seq, D_MODEL), hs.reshape(db, dec_seq, D_MODEL)) + tuple(jnp.stack(o) for o in outs)
```
